```python
import math
import jax
import jax.numpy as jnp
from jax import lax
import numpy as np

D_MODEL = 2048
BATCH = 1
SEQ = 16384
DEPTH = 4

HEAD_DIM = 64
D_MIX = D_MODEL
QB = 128
A_HEADS = D_MIX // 4 // (2 * HEAD_DIM)
A_DV = 2 * HEAD_DIM
B_HEADS = D_MIX // 4 // HEAD_DIM
DILATED_PATTERNS = ((128, 1), (512, 4), (2048, 16))
C_HEADS = D_MIX // 2 // HEAD_DIM
C_GROUPS = 2
C_HPG = C_HEADS // C_GROUPS
CMP_LEN = 32
CMP_STRIDE = 16
CMP_HID = 2 * HEAD_DIM
SLC_LEN = 64
N_SELECT = 16
WIN = 512
N_BRANCH = 3
D_FF = 5632
N_EXPERTS = 8
TOP_K = 2
EPS = 1e-6
NEG_INF = -1e30
FORCE_SCORE = 1e6

IN_SIZES = (
    A_HEADS * HEAD_DIM, A_HEADS * HEAD_DIM, A_HEADS * HEAD_DIM, A_HEADS * HEAD_DIM, A_HEADS * A_DV,
    B_HEADS * HEAD_DIM, B_HEADS * HEAD_DIM, B_HEADS * HEAD_DIM,
    C_HEADS * HEAD_DIM,
    C_GROUPS * HEAD_DIM, C_GROUPS * HEAD_DIM, C_GROUPS * HEAD_DIM,
    C_GROUPS * HEAD_DIM, C_GROUPS * HEAD_DIM, C_GROUPS * HEAD_DIM,
    C_HEADS * N_BRANCH,
)
IN_COLS = sum(IN_SIZES)
IN_OFFSETS = tuple(int(c) for c in np.cumsum(IN_SIZES)[:-1])

kernel_name = "hybrid_diff_dilated_nsa_moe_trunk"


def rmsnorm(x, g):
    xf = x.astype(jnp.float32)
    y = xf * lax.rsqrt(jnp.mean(xf * xf, axis=-1, keepdims=True) + EPS)
    return (y * g.astype(jnp.float32)).astype(x.dtype)


def alibi_slopes(n):
    return jnp.exp2(-8.0 * jnp.arange(1, n + 1, dtype=jnp.float32) / n)


def split_heads(t, n):
    return t.reshape(t.shape[:-1] + (n, t.shape[-1] // n))


def masked_softmax(s, valid):
    s = jnp.where(valid, s, NEG_INF)
    m = jnp.max(s, axis=-1, keepdims=True)
    e = jnp.where(valid, jnp.exp(s - m), 0.0)
    den = jnp.maximum(jnp.sum(e, axis=-1, keepdims=True), 1e-30)
    return e / den, (m + jnp.log(den))[..., 0]


def diff_attention(q1, q2, k1, k2, v, lam, lam_init, g_sub):
    B, S, H, dk = q1.shape
    dv = v.shape[-1]
    nb = S // QB
    slopes = alibi_slopes(H)
    qs = jnp.stack([q1, q2], axis=2).reshape(B, nb, QB, 2, H, dk).transpose(1, 0, 2, 3, 4, 5)
    ks = jnp.stack([k1, k2], axis=2)
    kpos = jnp.arange(S)

    def block(args):
        qblk, i = args
        qpos = i * QB + jnp.arange(QB)
        dist = qpos[:, None] - kpos[None, :]
        s = jnp.einsum("bqmhd,bkmhd->bmhqk", qblk, ks, preferred_element_type=jnp.float32) * dk ** -0.5
        s = s - slopes[:, None, None] * dist
        p, _ = masked_softmax(s, dist >= 0)
        a = p[:, 0] - lam * p[:, 1]
        return jnp.einsum("bhqk,bkhd->bqhd", a.astype(v.dtype), v)

    o = lax.map(block, (qs, jnp.arange(nb)))
    o = o.transpose(1, 0, 2, 3, 4).reshape(B, S, H, dv)
    o = rmsnorm(o, g_sub) * (1.0 - lam_init)
    return o.reshape(B, S, H * dv)


def banded_attention(q, k, v, nw, step, slopes):
    N, L, H, dh = q.shape
    nb = -(-L // QB)
    extra = nb * QB - L
    qb = jnp.pad(q, ((0, 0), (0, extra), (0, 0), (0, 0))).reshape(N, nb, QB, H, dh)
    pad_kv = ((0, 0), (nw, extra), (0, 0), (0, 0))
    idx = jnp.arange(nb)[:, None] * QB + jnp.arange(QB + nw)[None, :]
    kb = jnp.pad(k, pad_kv)[:, idx]
    vb = jnp.pad(v, pad_kv)[:, idx]
    rel = (jnp.arange(QB)[:, None] + nw) - jnp.arange(QB + nw)[None, :]
    kabs = idx - nw
    valid = (rel >= 0) & (rel <= nw) & (kabs >= 0)[:, None, :]
    s = jnp.einsum("nbqhd,nbkhd->nbhqk", qb, kb, preferred_element_type=jnp.float32) * dh ** -0.5
    s = s - slopes[:, None, None] * (rel * step)
    p, lse = masked_softmax(s, valid[None, :, None])
    o = jnp.einsum("nbhqk,nbkhd->nbqhd", p.astype(v.dtype), vb)
    o = o.reshape(N, nb * QB, H, dh)[:, :L]
    lse = lse.transpose(0, 1, 3, 2).reshape(N, nb * QB, H)[:, :L]
    return o, lse


def to_residue(t, dil):
    B, S = t.shape[:2]
    rest = t.shape[2:]
    perm = (0, 2, 1) + tuple(range(3, 3 + len(rest)))
    return t.reshape((B, S // dil, dil) + rest).transpose(perm).reshape((B * dil, S // dil) + rest)


def from_residue(t, B, dil):
    L = t.shape[1]
    rest = t.shape[2:]
    perm = (0, 2, 1) + tuple(range(3, 3 + len(rest)))
    return t.reshape((B, dil, L) + rest).transpose(perm).reshape((B, L * dil) + rest)


def dilated_attention(q, k, v):
    B, S, H, dh = q.shape
    slopes = alibi_slopes(H)
    outs, lses = [], []
    for window, dil in DILATED_PATTERNS:
        o, lse = banded_attention(to_residue(q, dil), to_residue(k, dil), to_residue(v, dil),
                                  window // dil, dil, slopes)
        outs.append(from_residue(o, B, dil))
        lses.append(from_residue(lse, B, dil))
    w = jax.nn.softmax(jnp.stack(lses, axis=0), axis=0)
    o = jnp.einsum("pbsh,pbshd->bshd", w.astype(q.dtype), jnp.stack(outs, axis=0))
    return o.reshape(B, S, H * dh)


def nsa_compress(t, pos, w1, w2):
    B, S, G, dk = t.shape
    c = t.reshape(B, S // CMP_STRIDE, CMP_STRIDE, G, dk)
    blk = jnp.concatenate([c[:, :-1], c[:, 1:]], axis=2) + pos[:, None, :]
    blk = blk.transpose(0, 1, 3, 2, 4).reshape(B, S // CMP_STRIDE - 1, G, CMP_LEN * dk)
    return jax.nn.silu(blk @ w1) @ w2


def cmp_to_slc(imp, n_slc):
    r = SLC_LEN // CMP_STRIDE
    lead = CMP_LEN // CMP_STRIDE - 1
    padded = jnp.pad(imp, [(0, 0)] * (imp.ndim - 1) + [(lead, r)])
    return sum(padded[..., o:o + r * n_slc:r] for o in range(r + lead))


def nsa_attention(q, k_cmp, v_cmp, k_slc, v_slc, k_win, v_win, gate_logits):
    B, S, H, dk = q.shape
    G = k_slc.shape[2]
    Hg = H // G
    nb = S // QB
    n_slc = S // SLC_LEN
    n_sel = min(N_SELECT, n_slc)
    n_cmp = k_cmp.shape[1]
    scale = dk ** -0.5
    slopes = alibi_slopes(H).reshape(G, Hg)
    cmp_end = jnp.arange(n_cmp) * CMP_STRIDE + CMP_LEN - 1
    k_blocks = k_slc.reshape(B, n_slc, SLC_LEN, G, dk).transpose(0, 3, 1, 2, 4)
    v_blocks = v_slc.reshape(B, n_slc, SLC_LEN, G, dk).transpose(0, 3, 1, 2, 4)
    kw = jnp.pad(k_win, ((0, 0), (WIN, 0), (0, 0), (0, 0)))
    vw = jnp.pad(v_win, ((0, 0), (WIN, 0), (0, 0), (0, 0)))
    gather = jax.vmap(jax.vmap(lambda blocks, ix: blocks[ix]))
    qs = q.reshape(B, nb, QB, G, Hg, dk).transpose(1, 0, 2, 3, 4, 5)
    gs = jax.nn.sigmoid(gate_logits.astype(jnp.float32)).reshape(B, nb, QB, G, Hg, N_BRANCH)
    gs = gs.transpose(1, 0, 2, 3, 4, 5)
    blk_ids = jnp.arange(n_slc)

    def block(args):
        qblk, gblk, i = args
        t = i * QB + jnp.arange(QB)
        d_c = t[:, None] - cmp_end[None, :]
        s = jnp.einsum("bqghd,bngd->bghqn", qblk, k_cmp, preferred_element_type=jnp.float32) * scale
        s = s - slopes[:, :, None, None] * d_c
        p_c, _ = masked_softmax(s, d_c >= 0)
        o_c = jnp.einsum("bghqn,bngd->bqghd", p_c.astype(q.dtype), v_cmp)
        imp = cmp_to_slc(jnp.sum(p_c, axis=2), n_slc)
        tb = t // SLC_LEN
        forced = (blk_ids[None, :] == 0) | (blk_ids[None, :] == tb[:, None]) | (blk_ids[None, :] == tb[:, None] - 1)
        allowed = blk_ids[None, :] * SLC_LEN <= t[:, None]
        score = jnp.where(allowed, jnp.where(forced, FORCE_SCORE, imp), -1.0)
        _, sel = lax.top_k(score, n_sel)
        ks = gather(k_blocks, sel).reshape(B, G, QB, n_sel * SLC_LEN, dk)
        vs = gather(v_blocks, sel).reshape(B, G, QB, n_sel * SLC_LEN, dk)
        pos = (sel[..., None] * SLC_LEN + jnp.arange(SLC_LEN)).reshape(B, G, QB, n_sel * SLC_LEN)
        d_s = t[None, None, :, None] - pos
        s = jnp.einsum("bqghd,bgqkd->bghqk", qblk, ks, preferred_element_type=jnp.float32) * scale
        s = s - slopes[None, :, :, None, None] * d_s[:, :, None]
        p_s, _ = masked_softmax(s, (d_s >= 0)[:, :, None])
        o_s = jnp.einsum("bghqk,bgqkd->bqghd", p_s.astype(q.dtype), vs)
        kwin = lax.dynamic_slice_in_dim(kw, i * QB, QB + WIN, axis=1)
        vwin = lax.dynamic_slice_in_dim(vw, i * QB, QB + WIN, axis=1)
        kpos = i * QB - WIN + jnp.arange(QB + WIN)
        d_w = t[:, None] - kpos[None, :]
        valid_w = (d_w >= 0) & (d_w < WIN) & (kpos >= 0)[None, :]
        s = jnp.einsum("bqghd,bkgd->bghqk", qblk, kwin, preferred_element_type=jnp.float32) * scale
        s = s - slopes[:, :, None, None] * d_w
        p_w, _ = masked_softmax(s, valid_w)
        o_w = jnp.einsum("bghqk,bkgd->bqghd", p_w.astype(q.dtype), vwin)
        o = gblk[..., 0:1] * o_c + gblk[..., 1:2] * o_s + gblk[..., 2:3] * o_w
        return o.astype(q.dtype)

    o = lax.map(block, (qs, gs, jnp.arange(nb)))
    return o.transpose(1, 0, 2, 3, 4, 5).reshape(B, S, H * dk)


def swiglu(h, wg, wu, wd):
    return (jax.nn.silu(h @ wg) * (h @ wu)) @ wd


def moe_swiglu(h, w_router, b_router, wg, wu, wd):
    logits = jnp.einsum("bsd,de->bse", h, w_router, preferred_element_type=jnp.float32)
    logits = logits + b_router.astype(jnp.float32)
    top_logit, top_idx = lax.top_k(logits, TOP_K)
    top_w = jax.nn.softmax(top_logit, axis=-1)
    combine = jnp.einsum("bsk,bske->bse", top_w, jax.nn.one_hot(top_idx, N_EXPERTS, dtype=jnp.float32))
    out = jnp.zeros(h.shape, jnp.float32)
    for e in range(N_EXPERTS):
        out = out + combine[..., e:e + 1] * swiglu(h, wg[e], wu[e], wd[e])
    return out.astype(h.dtype)


def setup_inputs(seed: int = 0) -> dict:
    key = jax.random.key(seed)
    keys = iter(jax.random.split(key, 32))

    def nrm(shape, scale):
        return scale * jax.random.normal(next(keys), shape, jnp.float32)

    n_dense = (DEPTH + 1) // 2
    n_moe = DEPTH // 2
    out_scale = (2 * DEPTH) ** -0.5
    return {
        "x": nrm((BATCH, SEQ, D_MODEL), 1.0),
        "ln_attn": 1.0 + nrm((DEPTH, D_MODEL), 0.02),
        "w_in": nrm((DEPTH, D_MODEL, IN_COLS), D_MODEL ** -0.5),
        "w_out": nrm((DEPTH, D_MIX, D_MODEL), D_MIX ** -0.5 * out_scale),
        "lam_q1": nrm((DEPTH, HEAD_DIM), 0.1),
        "lam_k1": nrm((DEPTH, HEAD_DIM), 0.1),
        "lam_q2": nrm((DEPTH, HEAD_DIM), 0.1),
        "lam_k2": nrm((DEPTH, HEAD_DIM), 0.1),
        "subln": 1.0 + nrm((DEPTH, A_DV), 0.02),
        "cmp_pos_k": nrm((DEPTH, CMP_LEN, HEAD_DIM), 0.1),
        "cmp_w1_k": nrm((DEPTH, CMP_LEN * HEAD_DIM, CMP_HID), (CMP_LEN * HEAD_DIM) ** -0.5),
        "cmp_w2_k": nrm((DEPTH, CMP_HID, HEAD_DIM), CMP_HID ** -0.5),
        "cmp_pos_v": nrm((DEPTH, CMP_LEN, HEAD_DIM), 0.1),
        "cmp_w1_v": nrm((DEPTH, CMP_LEN * HEAD_DIM, CMP_HID), (CMP_LEN * HEAD_DIM) ** -0.5),
        "cmp_w2_v": nrm((DEPTH, CMP_HID, HEAD_DIM), CMP_HID ** -0.5),
        "ln_ffn": 1.0 + nrm((DEPTH, D_MODEL), 0.02),
        "ffn_w_gate": nrm((n_dense, D_MODEL, D_FF), D_MODEL ** -0.5),
        "ffn_w_up": nrm((n_dense, D_MODEL, D_FF), D_MODEL ** -0.5),
        "ffn_w_down": nrm((n_dense, D_FF, D_MODEL), D_FF ** -0.5 * out_scale),
        "router_w": nrm((n_moe, D_MODEL, N_EXPERTS), D_MODEL ** -0.5),
        "router_b": nrm((n_moe, N_EXPERTS), 0.01),
        "exp_w_gate": nrm((n_moe, N_EXPERTS, D_MODEL, D_FF), D_MODEL ** -0.5),
        "exp_w_up": nrm((n_moe, N_EXPERTS, D_MODEL, D_FF), D_MODEL ** -0.5),
        "exp_w_down": nrm((n_moe, N_EXPERTS, D_FF, D_MODEL), D_FF ** -0.5 * out_scale),
        "ln_final": 1.0 + nrm((D_MODEL,), 0.02),
    }


def reference(x, ln_attn, w_in, w_out, lam_q1, lam_k1, lam_q2, lam_k2, subln,
              cmp_pos_k, cmp_w1_k, cmp_w2_k, cmp_pos_v, cmp_w1_v, cmp_w2_v,
              ln_ffn, ffn_w_gate, ffn_w_up, ffn_w_down,
              router_w, router_b, exp_w_gate, exp_w_up, exp_w_down, ln_final):
    B, S, _ = x.shape
    for l in range(DEPTH):
        h = rmsnorm(x, ln_attn[l])
        proj = jnp.einsum("bsd,dc->bsc", h, w_in[l])
        (aq1, aq2, ak1, ak2, av, bq, bk, bv, cq,
         ckc, cvc, cks, cvs, ckw, cvw, cg) = jnp.split(proj, IN_OFFSETS, axis=-1)
        lam_init = 0.8 - 0.6 * math.exp(-0.3 * l)
        lam = (jnp.exp(jnp.sum(lam_q1[l] * lam_k1[l]).astype(jnp.float32))
               - jnp.exp(jnp.sum(lam_q2[l] * lam_k2[l]).astype(jnp.float32)) + lam_init)
        oa = diff_attention(split_heads(aq1, A_HEADS), split_heads(aq2, A_HEADS),
                            split_heads(ak1, A_HEADS), split_heads(ak2, A_HEADS),
                            split_heads(av, A_HEADS), lam, lam_init, subln[l])
        ob = dilated_attention(split_heads(bq, B_HEADS), split_heads(bk, B_HEADS), split_heads(bv, B_HEADS))
        kc = nsa_compress(split_heads(ckc, C_GROUPS), cmp_pos_k[l], cmp_w1_k[l], cmp_w2_k[l])
        vc = nsa_compress(split_heads(cvc, C_GROUPS), cmp_pos_v[l], cmp_w1_v[l], cmp_w2_v[l])
        oc = nsa_attention(split_heads(cq, C_HEADS), kc, vc,
                           split_heads(cks, C_GROUPS), split_heads(cvs, C_GROUPS),
                           split_heads(ckw, C_GROUPS), split_heads(cvw, C_GROUPS),
                           cg.reshape(B, S, C_HEADS, N_BRANCH))
        mix = jnp.concatenate([oa, ob, oc], axis=-1)
        x = x + jnp.einsum("bsc,cd->bsd", mix, w_out[l])
        h = rmsnorm(x, ln_ffn[l])
        j = l // 2
        if l % 2 == 0:
            x = x + swiglu(h, ffn_w_gate[j], ffn_w_up[j], ffn_w_down[j])
        else:
            x = x + moe_swiglu(h, router_w[j], router_b[j], exp_w_gate[j], exp_w_up[j], exp_w_down[j])
    return rmsnorm(x, ln_final)
```

```python
import functools
import math

import numpy as np
import jax
import jax.numpy as jnp
from jax import lax
from jax.experimental import pallas as pl
from jax.experimental.pallas import tpu as pltpu

F32 = jnp.float32
BF16 = jnp.bfloat16

HEAD_DIM = 64
LANE = 128
A_HEADS = 4
B_HEADS = 8
C_HEADS = 16
C_GROUPS = 2
C_PAIRS = C_HEADS // C_GROUPS
N_BRANCH = 3
CMP_LEN = 32
CMP_STRIDE = 16
SLC_LEN = 64
N_SELECT = 16
WIN = 512
DILATED_PATTERNS = ((128, 1), (512, 4), (2048, 16))
N_EXPERTS = 8
EPS = 1e-6
NEG = -1e30
FORCE_SCORE = 1e6
QK_SCALE = HEAD_DIM ** -0.5

BLK_AQ, BLK_AK, BLK_AV = 0, 4, 8
BLK_BQ, BLK_BK, BLK_BV = 12, 16, 20
BLK_CQ = 24
BLK_CKC, BLK_CVC, BLK_CKS, BLK_CVS, BLK_CKW, BLK_CVW, BLK_CG = 32, 33, 34, 35, 36, 37, 38
N_BLK = 40

TQ = 256
TQ_CMP = 128
VMEM_LIMIT = 56 * 1024 * 1024


def _cparams(sem):
    return pltpu.CompilerParams(dimension_semantics=sem, vmem_limit_bytes=VMEM_LIMIT)


def _in_col_perm(d_mix):
    a_w = A_HEADS * HEAD_DIM
    sizes = [a_w, a_w, a_w, a_w, A_HEADS * 2 * HEAD_DIM,
             B_HEADS * HEAD_DIM, B_HEADS * HEAD_DIM, B_HEADS * HEAD_DIM,
             C_HEADS * HEAD_DIM] + [C_GROUPS * HEAD_DIM] * 6 + [C_HEADS * N_BRANCH]
    off = np.concatenate([[0], np.cumsum(sizes)])
    (aq1, aq2, ak1, ak2, av, bq, bk, bv, cq, ckc, cvc, cks, cvs, ckw, cvw, cg) = off[:-1]
    r = np.arange
    cols = []
    for h in range(A_HEADS):
        cols += [aq1 + h * 64 + r(64), aq2 + h * 64 + r(64)]
    for h in range(A_HEADS):
        cols += [ak1 + h * 64 + r(64), ak2 + h * 64 + r(64)]
    for h in range(A_HEADS):
        cols += [av + h * 128 + r(128)]
    for base in (bq, bk, bv):
        cols += [base + r(B_HEADS * HEAD_DIM)]
    for j in range(C_PAIRS):
        cols += [cq + j * 64 + r(64), cq + (C_PAIRS + j) * 64 + r(64)]
    for base in (ckc, cvc, cks, cvs, ckw, cvw):
        cols += [base + r(128)]
    cols += [cg + r(C_HEADS * N_BRANCH), -np.ones(LANE - C_HEADS * N_BRANCH, np.int64)]
    cols += [-np.ones(LANE, np.int64)]
    cols = np.concatenate(cols)
    assert cols.shape[0] == N_BLK * LANE and off[-1] == cols.max() + 1
    return cols


def _out_row_perm():
    rows = [np.arange(A_HEADS * 2 * HEAD_DIM + B_HEADS * HEAD_DIM)]
    c0 = rows[0].shape[0]
    for j in range(C_PAIRS):
        rows += [c0 + j * 64 + np.arange(64), c0 + (C_PAIRS + j) * 64 + np.arange(64)]
    return np.concatenate(rows)


def _alibi(n):
    return np.exp2(-8.0 * np.arange(1, n + 1, dtype=np.float64) / n).astype(np.float32)


def _causal_table(t):
    d = np.arange(t)[:, None] - np.arange(t)[None, :]
    return np.where(d >= 0, 0.0, NEG).astype(np.float32)[None]


def _dilated_table(t):
    n_ob = max(w for w, _ in DILATED_PATTERNS) // t + 1
    tabs = []
    for ob in range(n_ob):
        d = ob * t + np.arange(t)[:, None] - np.arange(t)[None, :]
        c = np.zeros(d.shape, np.float64)
        for w, dil in DILATED_PATTERNS:
            c += (d >= 0) & (d <= w) & (d % dil == 0)
        tabs.append(np.where(c > 0, np.log(np.maximum(c, 1.0)), NEG))
    return np.stack(tabs).astype(np.float32)


def _window_table(t):
    n_ob = (WIN - 1 + t - 1) // t + 1
    tabs = []
    for ob in range(n_ob):
        d = ob * t + np.arange(t)[:, None] - np.arange(t)[None, :]
        tabs.append(np.where((d >= 0) & (d < WIN), 0.0, NEG))
    return np.stack(tabs).astype(np.float32)


def _select_expand_table(t):
    per = t // SLC_LEN
    n = LANE // per
    e = np.zeros((n, LANE, t), np.float32)
    for jj in range(n):
        e[jj, jj * per + np.arange(t) // SLC_LEN, np.arange(t)] = 1e30
    return e


def _gate_expand_table():
    e = np.zeros((N_BRANCH, C_PAIRS, LANE, LANE), np.float32)
    for br in range(N_BRANCH):
        for j in range(C_PAIRS):
            e[br, j, j * N_BRANCH + br, :64] = 1.0
            e[br, j, (C_PAIRS + j) * N_BRANCH + br, 64:] = 1.0
    return e


def _cmp_to_slc_table(n_cmp, n_slc):
    r = SLC_LEN // CMP_STRIDE
    m = np.zeros((n_cmp, n_slc), np.float32)
    for b in range(n_slc):
        for n in range(r * b - 1, r * b + r):
            if 0 <= n < n_cmp:
                m[n, b] = 1.0
    return m


def _rms(x, g):
    return x * lax.rsqrt(jnp.mean(x * x, axis=-1, keepdims=True) + EPS) * g


def _split_halves(q):
    lane = lax.broadcasted_iota(jnp.int32, q.shape, 1)
    zero = jnp.zeros_like(q)
    return jnp.concatenate([jnp.where(lane < HEAD_DIM, q, zero),
                            jnp.where(lane >= HEAD_DIM, q, zero)], axis=0)


def _merge_halves(o2, t):
    lane = lax.broadcasted_iota(jnp.int32, (t, LANE), 1)
    return jnp.where(lane < HEAD_DIM, o2[:t], o2[t:])


def _dot_nt(a, b):
    return lax.dot_general(a, b, (((1,), (1,)), ((), ())), preferred_element_type=F32)


def _dot_hilo(x, w):
    hi = x.astype(BF16)
    lo = (x - hi.astype(F32)).astype(BF16)
    return (jnp.dot(hi, w, preferred_element_type=F32)
            + jnp.dot(lo, w, preferred_element_type=F32))


def _gate_lanes(cg, eg):
    return _dot_hilo(jax.nn.sigmoid(cg.astype(F32)), eg)


def _inproj_kernel(x_ref, g_ref, w_ref, o_ref, h_ref, *, nb):
    @pl.when(pl.program_id(1) == 0)
    def _():
        h_ref[...] = _rms(x_ref[...], g_ref[...]).astype(BF16)

    h = h_ref[...]
    for c in range(nb // 2):
        r = jnp.dot(h, w_ref[:, c * 256:(c + 1) * 256], preferred_element_type=F32)
        o_ref[2 * c] = r[:, :LANE].astype(BF16)
        o_ref[2 * c + 1] = r[:, LANE:].astype(BF16)


def _inproj(x, g, w):
    s, d = x.shape
    tm, nb = 1024, 10
    tm = min(tm, s)
    return pl.pallas_call(
        functools.partial(_inproj_kernel, nb=nb),
        grid=(s // tm, N_BLK // nb),
        in_specs=[pl.BlockSpec((tm, d), lambda i, j: (i, 0)),
                  pl.BlockSpec((1, d), lambda i, j: (0, 0)),
                  pl.BlockSpec((d, nb * LANE), lambda i, j: (0, j))],
        out_specs=pl.BlockSpec((nb, tm, LANE), lambda i, j: (j, i, 0)),
        out_shape=jax.ShapeDtypeStruct((N_BLK, s, LANE), BF16),
        scratch_shapes=[pltpu.VMEM((tm, d), BF16)],
        compiler_params=_cparams(("parallel", "arbitrary")),
        name="inproj",
    )(x, g, w)


def _flash_kernel(*refs, mode, npi, tq, n_tab, max_ob, lam_init):
    it = iter(refs)
    slope_ref = next(it)
    q_ref, k_ref, v_ref, tab_ref = next(it), next(it), next(it), next(it)
    if mode == "diff":
        lam_ref, gsub_ref = next(it), next(it)
    if mode in ("select", "window"):
        cg_ref, eg_ref = next(it), next(it)
    if mode == "select":
        sel_ref, ex_ref = next(it), next(it)
    o_ref = next(it)
    m_ref, l_ref, acc_ref = next(it), next(it), next(it)

    p = pl.program_id(0)
    i = pl.program_id(1)
    nblk = 2 * npi
    q2 = jnp.concatenate([_split_halves(q_ref[b]) for b in range(npi)], axis=0)
    q2 = q2 * jnp.asarray(QK_SCALE, BF16)
    slopes = [slope_ref[p * nblk + b] for b in range(nblk)]

    m_ref[...] = jnp.full(m_ref.shape, NEG, F32)
    l_ref[...] = jnp.zeros(l_ref.shape, F32)
    acc_ref[...] = jnp.zeros(acc_ref.shape, F32)

    def step(j, ob):
        ks = pl.multiple_of(j * tq, tq)
        k = k_ref[pl.ds(ks, tq), :]
        v = v_ref[pl.ds(ks, tq), :]
        s = _dot_nt(q2, k)
        colv = (lax.broadcasted_iota(jnp.int32, (1, tq), 1) + (j - i) * tq).astype(F32)
        if mode == "select":
            per = LANE * SLC_LEN // tq
            half = j // per
            selcat = jnp.concatenate([sel_ref[0, half], sel_ref[1, half]], axis=0)
            maskb = jnp.dot(selcat, ex_ref[j % per], preferred_element_type=F32)
        pieces = []
        for b in range(nblk):
            sb = s[b * tq:(b + 1) * tq] + slopes[b] * colv
            if ob is not None:
                sb = sb + tab_ref[ob]
            if mode == "select":
                sb = sb + maskb[(b % 2) * tq:(b % 2 + 1) * tq]
            pieces.append(sb)
        s = jnp.concatenate(pieces, axis=0)
        m_prev = m_ref[...]
        m_new = jnp.maximum(m_prev, jnp.max(s, axis=1, keepdims=True))
        alpha = jnp.exp(m_prev - m_new)
        pr = jnp.exp(s - m_new)
        l_ref[...] = alpha * l_ref[...] + jnp.sum(pr, axis=1, keepdims=True)
        acc_ref[...] = alpha * acc_ref[...] + jnp.dot(pr.astype(BF16), v,
                                                      preferred_element_type=F32)
        m_ref[...] = m_new

    j_lo = 0 if max_ob is None else jnp.maximum(i - max_ob, 0)
    j_mid = jnp.maximum(i - n_tab + 1, j_lo)

    def body(j, c):
        step(j, None)
        return c

    lax.fori_loop(j_lo, j_mid, body, 0)
    for ob in reversed(range(n_tab)):
        @pl.when(i - ob >= 0)
        def _(ob=ob):
            step(i - ob, ob)

    o2 = acc_ref[...] / l_ref[...]
    if mode == "diff":
        lam = (jnp.exp(jnp.sum(lam_ref[0:1] * lam_ref[1:2], keepdims=True))
               - jnp.exp(jnp.sum(lam_ref[2:3] * lam_ref[3:4], keepdims=True)) + lam_init)
        o = o2[:tq] - lam * o2[tq:]
        o_ref[0] = (_rms(o, gsub_ref[...]) * (1.0 - lam_init)).astype(BF16)
    else:
        for b in range(npi):
            o = _merge_halves(o2[2 * b * tq:(2 * b + 2) * tq], tq)
            if mode in ("select", "window"):
                o = o * _gate_lanes(cg_ref[...], eg_ref[b])
            o_ref[b] = o.astype(BF16)


def _flash(proj, slopes, tab, *, mode, qblk, kblk, vblk, npi, n_outer, kv_per_outer,
           max_ob=None, extra=(), extra_specs=(), lam_init=0.0):
    _, s, _ = proj.shape
    tq = min(TQ, s)
    n_tab = tab.shape[0]
    rows = 2 * npi * tq
    kvmap = (lambda b: (lambda p, i: (b + p, 0, 0))) if kv_per_outer else (lambda b: (lambda p, i: (b, 0, 0)))
    in_specs = [pl.BlockSpec(memory_space=pltpu.SMEM),
                pl.BlockSpec((npi, tq, LANE), lambda p, i: (qblk // npi + p, i, 0)),
                pl.BlockSpec((None, s, LANE), kvmap(kblk)),
                pl.BlockSpec((None, s, LANE), kvmap(vblk)),
                pl.BlockSpec(tab.shape, lambda p, i: (0, 0, 0))] + list(extra_specs)
    return pl.pallas_call(
        functools.partial(_flash_kernel, mode=mode, npi=npi, tq=tq, n_tab=n_tab,
                          max_ob=max_ob, lam_init=lam_init),
        grid=(n_outer, s // tq),
        in_specs=in_specs,
        out_specs=pl.BlockSpec((npi, tq, LANE), lambda p, i: (p, i, 0)),
        out_shape=jax.ShapeDtypeStruct((n_outer * npi, s, LANE), BF16),
        scratch_shapes=[pltpu.VMEM((rows, 1), F32), pltpu.VMEM((rows, 1), F32),
                        pltpu.VMEM((rows, LANE), F32)],
        compiler_params=_cparams(("parallel", "arbitrary")),
        name="flash_" + mode,
    )(slopes, proj, proj, proj, tab, *extra)


def _compress_kernel(c_ref, pos_ref, w1_ref, w2_ref, o_ref):
    c = c_ref[0].astype(F32)
    half = c.shape[1]
    pos = pos_ref[0]
    xa = (c + pos[:, :half]).astype(BF16)
    xb = (c + pos[:, half:]).astype(BF16)
    u = jnp.dot(xa, w1_ref[0, :half], preferred_element_type=F32)
    v = jnp.dot(xb, w1_ref[0, half:], preferred_element_type=F32)
    hid = u + pltpu.roll(v, shift=v.shape[0] - 1, axis=0)
    act = hid * jax.nn.sigmoid(hid)
    o_ref[0] = jnp.dot(act.astype(BF16), w2_ref[0], preferred_element_type=F32)


def _compress(cin, pos, w1, w2):
    n, nc, width = cin.shape
    return pl.pallas_call(
        _compress_kernel,
        grid=(n,),
        in_specs=[pl.BlockSpec((1, nc, width), lambda a: (a, 0, 0)),
                  pl.BlockSpec((1, 1, 2 * width), lambda a: (a // 2, 0, 0)),
                  pl.BlockSpec((1, 2 * width, LANE), lambda a: (a // 2, 0, 0)),
                  pl.BlockSpec((1, LANE, LANE), lambda a: (a // 2, 0, 0))],
        out_specs=pl.BlockSpec((1, nc, LANE), lambda a: (a, 0, 0)),
        out_shape=jax.ShapeDtypeStruct((n, nc, LANE), F32),
        compiler_params=_cparams(("parallel",)),
        name="nsa_compress",
    )(cin, pos, w1, w2)


def _cmp_select_kernel(slope_ref, q_ref, kc_ref, vc_ref, cg_ref, eg_ref, m_ref,
                       o_ref, sel_ref, *, tq, n_cmp, n_slc, n_sel):
    i = pl.program_id(0)
    t = i * tq + lax.broadcasted_iota(jnp.int32, (tq, 1), 0)
    cmp_end = lax.broadcasted_iota(jnp.int32, (1, n_cmp), 1) * CMP_STRIDE + (CMP_LEN - 1)
    d_c = t - cmp_end
    valid = d_c >= 0
    d_cf = d_c.astype(F32)
    kc = kc_ref[...]
    vc = vc_ref[...]
    imp = [jnp.zeros((tq, n_cmp), F32), jnp.zeros((tq, n_cmp), F32)]
    for b in range(C_PAIRS):
        q2 = _split_halves(q_ref[b]) * jnp.asarray(QK_SCALE, BF16)
        s2 = _dot_nt(q2, kc)
        probs = []
        for hf in range(2):
            s = s2[hf * tq:(hf + 1) * tq] - slope_ref[2 * b + hf] * d_cf
            s = jnp.where(valid, s, NEG)
            mx = jnp.max(s, axis=1, keepdims=True)
            e = jnp.where(valid, jnp.exp(s - mx), 0.0)
            den = jnp.maximum(jnp.sum(e, axis=1, keepdims=True), 1e-30)
            pr = e / den
            imp[hf] = imp[hf] + pr
            probs.append(pr.astype(BF16))
        o2 = jnp.dot(jnp.concatenate(probs, axis=0), vc, preferred_element_type=F32)
        o = _merge_halves(o2, tq) * _gate_lanes(cg_ref[...], eg_ref[b])
        o_ref[b] = o.astype(BF16)

    imp2 = jnp.concatenate(imp, axis=0)
    slc = _dot_hilo(imp2, m_ref[...])
    t2 = jnp.concatenate([t, t], axis=0)
    blk = lax.broadcasted_iota(jnp.int32, (2 * tq, n_slc), 1)
    tb = jnp.right_shift(t2, int(math.log2(SLC_LEN)))
    forced = (blk == 0) | (blk == tb) | (blk == tb - 1)
    allowed = blk * SLC_LEN <= t2
    score = jnp.where(allowed, jnp.where(forced, FORCE_SCORE, slc), -1.0)
    picked = jnp.zeros(score.shape, jnp.int32)
    for _ in range(n_sel):
        mx = jnp.max(score, axis=1, keepdims=True)
        first = jnp.min(jnp.where(score == mx, blk, n_slc), axis=1, keepdims=True)
        hit = blk == first
        picked = jnp.where(hit, 1, picked)
        score = jnp.where(hit, -2.0, score)
    selm = jnp.where(picked > 0, 0.0, -1.0).astype(BF16)
    for g in range(C_GROUPS):
        for hf in range(n_slc // LANE):
            sel_ref[g, hf] = selm[g * tq:(g + 1) * tq, hf * LANE:(hf + 1) * LANE]


def _cmp_select(proj, kc, vc, slopes, eg, mtab):
    _, s, _ = proj.shape
    tq = min(TQ_CMP, s)
    n_cmp = kc.shape[0]
    n_slc = mtab.shape[1]
    return pl.pallas_call(
        functools.partial(_cmp_select_kernel, tq=tq, n_cmp=n_cmp, n_slc=n_slc,
                          n_sel=min(N_SELECT, s // SLC_LEN)),
        grid=(s // tq,),
        in_specs=[pl.BlockSpec(memory_space=pltpu.SMEM),
                  pl.BlockSpec((C_PAIRS, tq, LANE), lambda i: (BLK_CQ // C_PAIRS, i, 0)),
                  pl.BlockSpec((n_cmp, LANE), lambda i: (0, 0)),
                  pl.BlockSpec((n_cmp, LANE), lambda i: (0, 0)),
                  pl.BlockSpec((None, tq, LANE), lambda i: (BLK_CG, i, 0)),
                  pl.BlockSpec((C_PAIRS, LANE, LANE), lambda i: (0, 0, 0)),
                  pl.BlockSpec((n_cmp, n_slc), lambda i: (0, 0))],
        out_specs=[pl.BlockSpec((C_PAIRS, tq, LANE), lambda i: (0, i, 0)),
                   pl.BlockSpec((C_GROUPS, n_slc // LANE, tq, LANE), lambda i: (0, 0, i, 0))],
        out_shape=[jax.ShapeDtypeStruct((C_PAIRS, s, LANE), BF16),
                   jax.ShapeDtypeStruct((C_GROUPS, n_slc // LANE, s, LANE), BF16)],
        compiler_params=_cparams(("parallel",)),
        name="nsa_cmp_select",
    )(slopes, proj, kc, vc, proj, eg, mtab)


def _outproj_kernel(x_ref, oa_ref, ob_ref, oc_ref, os_ref, ow_ref, w_ref, o_ref):
    parts = [oa_ref[b] for b in range(oa_ref.shape[0])]
    parts += [ob_ref[b] for b in range(ob_ref.shape[0])]
    for b in range(oc_ref.shape[0]):
        parts.append((oc_ref[b].astype(F32) + os_ref[b].astype(F32)
                      + ow_ref[b].astype(F32)).astype(BF16))
    mix = jnp.concatenate(parts, axis=1)
    n = o_ref.shape[1]
    for c in range(n // 512):
        sl = slice(c * 512, (c + 1) * 512)
        o_ref[:, sl] = x_ref[:, sl] + jnp.dot(mix, w_ref[:, sl], preferred_element_type=F32)


def _outproj(x, oa, ob, oc, os_, ow, w):
    s, d = x.shape
    tm = min(512, s)
    blk = lambda a: pl.BlockSpec((a.shape[0], tm, LANE), lambda i: (0, i, 0))
    return pl.pallas_call(
        _outproj_kernel,
        grid=(s // tm,),
        in_specs=[pl.BlockSpec((tm, d), lambda i: (i, 0)), blk(oa), blk(ob), blk(oc), blk(os_), blk(ow),
                  pl.BlockSpec(w.shape, lambda i: (0, 0))],
        out_specs=pl.BlockSpec((tm, d), lambda i: (i, 0)),
        out_shape=jax.ShapeDtypeStruct((s, d), F32),
        compiler_params=_cparams(("parallel",)),
        name="outproj",
    )(x, oa, ob, oc, os_, ow, w)


def _ffn_kernel(te_ref, nt_ref, x_ref, g_ref, wg_ref, wu_ref, wd_ref, o_ref, h_ref, acc_ref, *, residual):
    t = pl.program_id(0)
    j = pl.program_id(1)

    @pl.when(t < nt_ref[0])
    def _():
        @pl.when(j == 0)
        def _():
            h_ref[...] = _rms(x_ref[...], g_ref[...]).astype(BF16)
            acc_ref[...] = jnp.zeros(acc_ref.shape, F32)

        h = h_ref[...]
        a = jnp.dot(h, wg_ref[0], preferred_element_type=F32)
        u = jnp.dot(h, wu_ref[0], preferred_element_type=F32)
        z = (a * jax.nn.sigmoid(a) * u).astype(BF16)
        acc_ref[...] += jnp.dot(z, wd_ref[0], preferred_element_type=F32)

        @pl.when(j == pl.num_programs(1) - 1)
        def _():
            if residual:
                o_ref[...] = x_ref[...] + acc_ref[...]
            else:
                o_ref[...] = acc_ref[...]

    @pl.when(t >= nt_ref[0])
    def _():
        o_ref[...] = jnp.zeros(o_ref.shape, F32)


def _ffn(x, g, wg, wu, wd, tile_expert, n_tiles, *, residual, tm):
    r, d = x.shape
    f = wg.shape[2]
    tf = 512 if f % 512 == 0 else f
    grid_spec = pltpu.PrefetchScalarGridSpec(
        num_scalar_prefetch=2,
        grid=(r // tm, f // tf),
        in_specs=[pl.BlockSpec((tm, d), lambda t, j, te, nt: (t, 0)),
                  pl.BlockSpec((1, d), lambda t, j, te, nt: (0, 0)),
                  pl.BlockSpec((1, d, tf), lambda t, j, te, nt: (te[t], 0, j)),
                  pl.BlockSpec((1, d, tf), lambda t, j, te, nt: (te[t], 0, j)),
                  pl.BlockSpec((1, tf, d), lambda t, j, te, nt: (te[t], j, 0))],
        out_specs=pl.BlockSpec((tm, d), lambda t, j, te, nt: (t, 0)),
        scratch_shapes=[pltpu.VMEM((tm, d), BF16), pltpu.VMEM((tm, d), F32)])
    return pl.pallas_call(
        functools.partial(_ffn_kernel, residual=residual),
        grid_spec=grid_spec,
        out_shape=jax.ShapeDtypeStruct((r, d), F32),
        compiler_params=_cparams(("parallel", "arbitrary")),
        name="ffn",
    )(tile_expert, n_tiles, x, g, wg, wu, wd)


def _router_kernel(x_ref, g_ref, w_ref, b_ref, o_ref):
    h = _rms(x_ref[...], g_ref[...])
    w = w_ref[...]
    w_hi = w.astype(BF16)
    w_lo = (w - w_hi.astype(F32)).astype(BF16)
    h_hi = h.astype(BF16)
    h_lo = (h - h_hi.astype(F32)).astype(BF16)
    logits = (jnp.dot(h_hi, w_hi, preferred_element_type=F32)
              + jnp.dot(h_lo, w_hi, preferred_element_type=F32)
              + jnp.dot(h_hi, w_lo, preferred_element_type=F32)) + b_ref[...]
    lane = lax.broadcasted_iota(jnp.int32, logits.shape, 1)
    logits = jnp.where(lane < N_EXPERTS, logits, NEG)
    l1 = jnp.max(logits, axis=1, keepdims=True)
    i1 = jnp.min(jnp.where(logits == l1, lane, LANE), axis=1, keepdims=True)
    rest = jnp.where(lane == i1, NEG, logits)
    l2 = jnp.max(rest, axis=1, keepdims=True)
    i2 = jnp.min(jnp.where(rest == l2, lane, LANE), axis=1, keepdims=True)
    e2 = jnp.exp(l2 - l1)
    w1 = 1.0 / (1.0 + e2)
    w2 = e2 / (1.0 + e2)
    out = jnp.where(lane == 0, i1.astype(F32),
                    jnp.where(lane == 1, i2.astype(F32),
                              jnp.where(lane == 2, w1, jnp.where(lane == 3, w2, 0.0))))
    o_ref[...] = out


def _router(x, g, w, b):
    s, d = x.shape
    tm = min(512, s)
    return pl.pallas_call(
        _router_kernel,
        grid=(s // tm,),
        in_specs=[pl.BlockSpec((tm, d), lambda i: (i, 0)),
                  pl.BlockSpec((1, d), lambda i: (0, 0)),
                  pl.BlockSpec((d, LANE), lambda i: (0, 0)),
                  pl.BlockSpec((1, LANE), lambda i: (0, 0))],
        out_specs=pl.BlockSpec((tm, LANE), lambda i: (i, 0)),
        out_shape=jax.ShapeDtypeStruct((s, LANE), F32),
        compiler_params=_cparams(("parallel",)),
        name="router",
    )(x, g, w, b)


def _row_copy(src_hbm, row, dst_ref, r, sem):
    return pltpu.make_async_copy(src_hbm.at[pl.ds(row, 1), :], dst_ref.at[pl.ds(r, 1), :], sem)


def _gather_kernel(idx_ref, x_hbm, o_ref, sem, *, tm):
    base = pl.program_id(0) * tm

    def issue(r, c):
        _row_copy(x_hbm, idx_ref[base + r], o_ref, r, sem).start()
        return c

    lax.fori_loop(0, tm, issue, 0)

    def wait(r, c):
        _row_copy(x_hbm, 0, o_ref, r, sem).wait()
        return c

    lax.fori_loop(0, tm, wait, 0)


def _gather_rows(x, idx, tm):
    d = x.shape[1]
    n = idx.shape[0]
    grid_spec = pltpu.PrefetchScalarGridSpec(
        num_scalar_prefetch=1,
        grid=(n // tm,),
        in_specs=[pl.BlockSpec(memory_space=pl.ANY)],
        out_specs=pl.BlockSpec((tm, d), lambda i, idx: (i, 0)),
        scratch_shapes=[pltpu.SemaphoreType.DMA(())])
    return pl.pallas_call(
        functools.partial(_gather_kernel, tm=tm),
        grid_spec=grid_spec,
        out_shape=jax.ShapeDtypeStruct((n, d), x.dtype),
        compiler_params=_cparams(("arbitrary",)),
        name="moe_gather",
    )(idx, x)


def _combine_kernel(d0_ref, d1_ref, x_ref, r_ref, y_hbm, o_ref, b0_ref, b1_ref, sem, *, tm):
    base = pl.program_id(0) * tm

    def issue(r, c):
        _row_copy(y_hbm, d0_ref[base + r], b0_ref, r, sem).start()
        _row_copy(y_hbm, d1_ref[base + r], b1_ref, r, sem).start()
        return c

    lax.fori_loop(0, tm, issue, 0)

    def wait(r, c):
        _row_copy(y_hbm, 0, b0_ref, r, sem).wait()
        _row_copy(y_hbm, 0, b1_ref, r, sem).wait()
        return c

    lax.fori_loop(0, tm, wait, 0)
    rt = r_ref[...]
    o_ref[...] = x_ref[...] + rt[:, 2:3] * b0_ref[...] + rt[:, 3:4] * b1_ref[...]


def _combine(x, route, y, d0, d1):
    s, d = x.shape
    tm = min(256, s)
    grid_spec = pltpu.PrefetchScalarGridSpec(
        num_scalar_prefetch=2,
        grid=(s // tm,),
        in_specs=[pl.BlockSpec((tm, d), lambda i, a, b: (i, 0)),
                  pl.BlockSpec((tm, LANE), lambda i, a, b: (i, 0)),
                  pl.BlockSpec(memory_space=pl.ANY)],
        out_specs=pl.BlockSpec((tm, d), lambda i, a, b: (i, 0)),
        scratch_shapes=[pltpu.VMEM((tm, d), F32), pltpu.VMEM((tm, d), F32),
                        pltpu.SemaphoreType.DMA(())])
    return pl.pallas_call(
        functools.partial(_combine_kernel, tm=tm),
        grid_spec=grid_spec,
        out_shape=jax.ShapeDtypeStruct((s, d), F32),
        compiler_params=_cparams(("arbitrary",)),
        name="moe_combine",
    )(d0, d1, x, route, y)


def _moe(x, g, w_router, b_router, wg, wu, wd):
    s, d = x.shape
    tm = min(512, s)
    wr = jnp.zeros((d, LANE), F32).at[:, :N_EXPERTS].set(w_router)
    br = jnp.zeros((1, LANE), F32).at[0, :N_EXPERTS].set(b_router)
    route = _router(x, g, wr, br)
    e = route[:, :2].astype(jnp.int32).reshape(-1)
    onehot = (e[:, None] == jnp.arange(N_EXPERTS)[None, :]).astype(jnp.int32)
    rank = jnp.sum((jnp.cumsum(onehot, axis=0) - onehot) * onehot, axis=1)
    counts = jnp.sum(onehot, axis=0)
    padded = ((counts + tm - 1) // tm) * tm
    ends = jnp.cumsum(padded)
    dest = (ends - padded)[e] + rank
    n_rows = 2 * s + N_EXPERTS * tm
    row_token = jnp.zeros((n_rows,), jnp.int32).at[dest].set(jnp.arange(2 * s, dtype=jnp.int32) // 2)
    tile_start = jnp.arange(n_rows // tm, dtype=jnp.int32) * tm
    tile_expert = jnp.minimum(jnp.sum(tile_start[:, None] >= ends[None, :], axis=1), N_EXPERTS - 1)
    n_tiles = (ends[-1] // tm).astype(jnp.int32).reshape(1)
    xs = _gather_rows(x, row_token, tm)
    ys = _ffn(xs, g, wg, wu, wd, tile_expert.astype(jnp.int32), n_tiles, residual=False, tm=tm)
    dest2 = dest.reshape(s, 2).astype(jnp.int32)
    return _combine(x, route, ys, dest2[:, 0], dest2[:, 1])


def _norm_kernel(x_ref, g_ref, o_ref):
    o_ref[...] = _rms(x_ref[...], g_ref[...])


def _final_norm(x, g):
    s, d = x.shape
    tm = min(512, s)
    return pl.pallas_call(
        _norm_kernel,
        grid=(s // tm,),
        in_specs=[pl.BlockSpec((tm, d), lambda i: (i, 0)), pl.BlockSpec((1, d), lambda i: (0, 0))],
        out_specs=pl.BlockSpec((tm, d), lambda i: (i, 0)),
        out_shape=jax.ShapeDtypeStruct((s, d), F32),
        compiler_params=_cparams(("parallel",)),
        name="final_norm",
    )(x, g)


def kernel(x, ln_attn, w_in, w_out, lam_q1, lam_k1, lam_q2, lam_k2, subln,
           cmp_pos_k, cmp_w1_k, cmp_w2_k, cmp_pos_v, cmp_w1_v, cmp_w2_v,
           ln_ffn, ffn_w_gate, ffn_w_up, ffn_w_down,
           router_w, router_b, exp_w_gate, exp_w_up, exp_w_down, ln_final):
    bsz, s, d = x.shape
    assert bsz == 1 and s % TQ == 0
    depth = w_in.shape[0]
    tq = min(TQ, s)
    n_cmp = s // CMP_STRIDE

    in_cols = _in_col_perm(d)
    in_valid = jnp.asarray((in_cols >= 0).astype(np.float32))
    in_cols = jnp.asarray(np.maximum(in_cols, 0))
    out_rows = jnp.asarray(_out_row_perm())

    slopes_a = jnp.asarray(np.repeat(_alibi(A_HEADS), 2))
    slopes_b = jnp.asarray(_alibi(B_HEADS))
    sl_c = _alibi(C_HEADS)
    slopes_c = jnp.asarray(np.stack([sl_c[:C_PAIRS], sl_c[C_PAIRS:]], axis=1).reshape(-1))
    tab_causal = jnp.asarray(_causal_table(tq))
    tab_dil = jnp.asarray(_dilated_table(tq))
    tab_win = jnp.asarray(_window_table(tq))
    ex_sel = jnp.asarray(_select_expand_table(tq), BF16)
    eg = jnp.asarray(_gate_expand_table(), BF16)
    n_slc_pad = -(-(s // SLC_LEN) // LANE) * LANE
    mtab = jnp.asarray(_cmp_to_slc_table(n_cmp, n_slc_pad), BF16)
    one_tile = jnp.zeros((s // min(512, s),), jnp.int32)

    cg_spec = pl.BlockSpec((None, tq, LANE), lambda p, i: (BLK_CG, i, 0))
    eg_spec = pl.BlockSpec((C_PAIRS, LANE, LANE), lambda p, i: (0, 0, 0))

    xs = x[0]
    for l in range(depth):
        w = (jnp.take(w_in[l], in_cols, axis=1) * in_valid[None, :]).astype(BF16)
        proj = _inproj(xs, ln_attn[l][None, :], w)

        lam_init = 0.8 - 0.6 * math.exp(-0.3 * l)
        lam_tab = jnp.zeros((8, LANE), F32).at[:4, :HEAD_DIM].set(
            jnp.stack([lam_q1[l], lam_k1[l], lam_q2[l], lam_k2[l]]))
        oa = _flash(proj, slopes_a, tab_causal, mode="diff", qblk=BLK_AQ, kblk=BLK_AK, vblk=BLK_AV,
                    npi=1, n_outer=A_HEADS, kv_per_outer=True, lam_init=lam_init,
                    extra=(lam_tab, subln[l][None, :]),
                    extra_specs=(pl.BlockSpec((8, LANE), lambda p, i: (0, 0)),
                                 pl.BlockSpec((1, LANE), lambda p, i: (0, 0))))

        ob = _flash(proj, slopes_b, tab_dil, mode="pair", qblk=BLK_BQ, kblk=BLK_BK, vblk=BLK_BV,
                    npi=1, n_outer=B_HEADS // 2, kv_per_outer=True, max_ob=tab_dil.shape[0] - 1)

        pc = proj[BLK_CKC:BLK_CVC + 1].reshape(2, n_cmp, CMP_STRIDE, C_GROUPS, HEAD_DIM)
        cin = pc.transpose(0, 3, 1, 2, 4).reshape(2 * C_GROUPS, n_cmp, CMP_STRIDE * HEAD_DIM)
        pos = jnp.stack([cmp_pos_k[l], cmp_pos_v[l]]).reshape(2, 1, CMP_LEN * HEAD_DIM)
        w1 = jnp.stack([cmp_w1_k[l], cmp_w1_v[l]]).astype(BF16)
        w2 = jnp.zeros((2, LANE, LANE), F32).at[:, :, :HEAD_DIM].set(
            jnp.stack([cmp_w2_k[l], cmp_w2_v[l]])).astype(BF16)
        cmp = _compress(cin, pos, w1, w2)
        kc = jnp.concatenate([cmp[0, :, :HEAD_DIM], cmp[1, :, :HEAD_DIM]], axis=1).astype(BF16)
        vc = jnp.concatenate([cmp[2, :, :HEAD_DIM], cmp[3, :, :HEAD_DIM]], axis=1).astype(BF16)
        oc, selm = _cmp_select(proj, kc, vc, slopes_c, eg[0], mtab)
        os_ = _flash(proj, slopes_c, tab_causal, mode="select", qblk=BLK_CQ, kblk=BLK_CKS, vblk=BLK_CVS,
                     npi=C_PAIRS, n_outer=1, kv_per_outer=False,
                     extra=(proj, eg[1], selm, ex_sel),
                     extra_specs=(cg_spec, eg_spec,
                                  pl.BlockSpec((C_GROUPS, selm.shape[1], tq, LANE), lambda p, i: (0, 0, i, 0)),
                                  pl.BlockSpec(ex_sel.shape, lambda p, i: (0, 0, 0))))
        ow = _flash(proj, slopes_c, tab_win, mode="window", qblk=BLK_CQ, kblk=BLK_CKW, vblk=BLK_CVW,
                    npi=C_PAIRS, n_outer=1, kv_per_outer=False, max_ob=tab_win.shape[0] - 1,
                    extra=(proj, eg[2]), extra_specs=(cg_spec, eg_spec))

        xs = _outproj(xs, oa, ob, oc, os_, ow, jnp.take(w_out[l], out_rows, axis=0).astype(BF16))

        j = l // 2
        g_ffn = ln_ffn[l][None, :]
        if l % 2 == 0:
            n_t = jnp.full((1,), one_tile.shape[0], jnp.int32)
            xs = _ffn(xs, g_ffn, ffn_w_gate[j][None].astype(BF16), ffn_w_up[j][None].astype(BF16),
                      ffn_w_down[j][None].astype(BF16), one_tile, n_t, residual=True, tm=min(512, s))
        else:
            xs = _moe(xs, g_ffn, router_w[j], router_b[j], exp_w_gate[j].astype(BF16),
                      exp_w_up[j].astype(BF16), exp_w_down[j].astype(BF16))
    return _final_norm(xs, ln_final[None, :])[None]
```

```python
import functools
import math

import numpy as np
import jax
import jax.numpy as jnp
from jax import lax
from jax.experimental import pallas as pl
from jax.experimental.pallas import tpu as pltpu

F32 = jnp.float32
BF16 = jnp.bfloat16

HEAD_DIM = 64
LANE = 128
A_HEADS = 4
B_HEADS = 8
C_HEADS = 16
C_GROUPS = 2
C_PAIRS = C_HEADS // C_GROUPS
N_BRANCH = 3
CMP_LEN = 32
CMP_STRIDE = 16
SLC_LEN = 64
N_SELECT = 16
WIN = 512
DILATED_PATTERNS = ((128, 1), (512, 4), (2048, 16))
N_EXPERTS = 8
EPS = 1e-6
NEG = -1e30
FORCE_SCORE = 1e6
QK_SCALE = HEAD_DIM ** -0.5

BLK_AQ, BLK_AK, BLK_AV = 0, 4, 8
BLK_BQ, BLK_BK, BLK_BV = 12, 16, 20
BLK_CQ = 24
BLK_CKC, BLK_CVC, BLK_CKS, BLK_CVS, BLK_CKW, BLK_CVW, BLK_CG = 32, 33, 34, 35, 36, 37, 38
N_BLK = 40

T_BIG = 512
T_WIN = 256
TQ_CMP = 128
VMEM_LIMIT = 56 * 1024 * 1024


def _cparams(sem):
    return pltpu.CompilerParams(dimension_semantics=sem, vmem_limit_bytes=VMEM_LIMIT)


def _in_col_perm(d_mix):
    a_w = A_HEADS * HEAD_DIM
    sizes = [a_w, a_w, a_w, a_w, A_HEADS * 2 * HEAD_DIM,
             B_HEADS * HEAD_DIM, B_HEADS * HEAD_DIM, B_HEADS * HEAD_DIM,
             C_HEADS * HEAD_DIM] + [C_GROUPS * HEAD_DIM] * 6 + [C_HEADS * N_BRANCH]
    off = np.concatenate([[0], np.cumsum(sizes)])
    (aq1, aq2, ak1, ak2, av, bq, bk, bv, cq, ckc, cvc, cks, cvs, ckw, cvw, cg) = off[:-1]
    r = np.arange
    cols = []
    for h in range(A_HEADS):
        cols += [aq1 + h * 64 + r(64), aq2 + h * 64 + r(64)]
    for h in range(A_HEADS):
        cols += [ak1 + h * 64 + r(64), ak2 + h * 64 + r(64)]
    for h in range(A_HEADS):
        cols += [av + h * 128 + r(128)]
    for base in (bq, bk, bv):
        cols += [base + r(B_HEADS * HEAD_DIM)]
    for j in range(C_PAIRS):
        cols += [cq + j * 64 + r(64), cq + (C_PAIRS + j) * 64 + r(64)]
    for base in (ckc, cvc, cks, cvs, ckw, cvw):
        cols += [base + r(128)]
    cols += [cg + r(C_HEADS * N_BRANCH), -np.ones(LANE - C_HEADS * N_BRANCH, np.int64)]
    cols += [-np.ones(LANE, np.int64)]
    cols = np.concatenate(cols)
    assert cols.shape[0] == N_BLK * LANE and off[-1] == cols.max() + 1
    return cols


def _out_row_perm():
    rows = [np.arange(A_HEADS * 2 * HEAD_DIM + B_HEADS * HEAD_DIM)]
    c0 = rows[0].shape[0]
    for j in range(C_PAIRS):
        rows += [c0 + j * 64 + np.arange(64), c0 + (C_PAIRS + j) * 64 + np.arange(64)]
    return np.concatenate(rows)


def _alibi(n):
    return np.exp2(-8.0 * np.arange(1, n + 1, dtype=np.float64) / n).astype(np.float32)


def _causal_table(t):
    d = np.arange(t)[:, None] - np.arange(t)[None, :]
    return np.where(d >= 0, 0.0, NEG).astype(np.float32)[None]


def _dilated_table(t):
    n_ob = max(w for w, _ in DILATED_PATTERNS) // t + 1
    tabs = []
    for ob in range(n_ob):
        d = ob * t + np.arange(t)[:, None] - np.arange(t)[None, :]
        c = np.zeros(d.shape, np.float64)
        for w, dil in DILATED_PATTERNS:
            c += (d >= 0) & (d <= w) & (d % dil == 0)
        tabs.append(np.where(c > 0, np.log(np.maximum(c, 1.0)), NEG))
    return np.stack(tabs).astype(np.float32)


def _window_table(t):
    n_ob = (WIN - 1 + t - 1) // t + 1
    tabs = []
    for ob in range(n_ob):
        d = ob * t + np.arange(t)[:, None] - np.arange(t)[None, :]
        tabs.append(np.where((d >= 0) & (d < WIN), 0.0, NEG))
    return np.stack(tabs).astype(np.float32)


def _select_expand_table(t):
    per = t // SLC_LEN
    n = LANE // per
    e = np.zeros((n, LANE, t), np.float32)
    for jj in range(n):
        e[jj, jj * per + np.arange(t) // SLC_LEN, np.arange(t)] = 1e30
    return e


def _gate_expand_table():
    e = np.zeros((N_BRANCH, C_PAIRS, LANE, LANE), np.float32)
    for br in range(N_BRANCH):
        for j in range(C_PAIRS):
            e[br, j, j * N_BRANCH + br, :64] = 1.0
            e[br, j, (C_PAIRS + j) * N_BRANCH + br, 64:] = 1.0
    return e


def _cmp_to_slc_table(n_cmp, n_slc):
    r = SLC_LEN // CMP_STRIDE
    m = np.zeros((n_cmp, n_slc), np.float32)
    for b in range(n_slc):
        for n in range(r * b - 1, r * b + r):
            if 0 <= n < n_cmp:
                m[n, b] = 1.0
    return m


def _rms(x, g):
    return x * lax.rsqrt(jnp.mean(x * x, axis=-1, keepdims=True) + EPS) * g


def _split_halves(q):
    lane = lax.broadcasted_iota(jnp.int32, q.shape, 1)
    zero = jnp.zeros_like(q)
    return jnp.concatenate([jnp.where(lane < HEAD_DIM, q, zero),
                            jnp.where(lane >= HEAD_DIM, q, zero)], axis=0)


def _merge_halves(o2, t):
    lane = lax.broadcasted_iota(jnp.int32, (t, LANE), 1)
    return jnp.where(lane < HEAD_DIM, o2[:t], o2[t:])


def _dot_nt(a, b):
    return lax.dot_general(a, b, (((1,), (1,)), ((), ())), preferred_element_type=F32)


def _dot_hilo(x, w):
    hi = x.astype(BF16)
    lo = (x - hi.astype(F32)).astype(BF16)
    return (jnp.dot(hi, w, preferred_element_type=F32)
            + jnp.dot(lo, w, preferred_element_type=F32))


def _gate_lanes(cg, eg):
    return _dot_hilo(jax.nn.sigmoid(cg.astype(F32)), eg)


def _inproj_kernel(x_ref, g_ref, w_ref, o_ref, h_ref, *, nb):
    @pl.when(pl.program_id(1) == 0)
    def _():
        h_ref[...] = _rms(x_ref[...], g_ref[...]).astype(BF16)

    h = h_ref[...]
    for c in range(nb // 2):
        r = jnp.dot(h, w_ref[:, c * 256:(c + 1) * 256], preferred_element_type=F32)
        o_ref[2 * c] = r[:, :LANE].astype(BF16)
        o_ref[2 * c + 1] = r[:, LANE:].astype(BF16)


def _inproj(x, g, w):
    s, d = x.shape
    tm, nb = 1024, 10
    tm = min(tm, s)
    return pl.pallas_call(
        functools.partial(_inproj_kernel, nb=nb),
        grid=(s // tm, N_BLK // nb),
        in_specs=[pl.BlockSpec((tm, d), lambda i, j: (i, 0)),
                  pl.BlockSpec((1, d), lambda i, j: (0, 0)),
                  pl.BlockSpec((d, nb * LANE), lambda i, j: (0, j))],
        out_specs=pl.BlockSpec((nb, tm, LANE), lambda i, j: (j, i, 0)),
        out_shape=jax.ShapeDtypeStruct((N_BLK, s, LANE), BF16),
        scratch_shapes=[pltpu.VMEM((tm, d), BF16)],
        compiler_params=_cparams(("parallel", "arbitrary")),
        name="inproj",
    )(x, g, w)


def _flash_kernel(*refs, mode, npi, t, nt, n_tab, max_ob):
    it = iter(refs)
    flag_ref, slope_ref = next(it), next(it)
    q_ref, k_ref, v_ref, tab_ref = next(it), next(it), next(it), next(it)
    if mode == "diff":
        lam_ref, gsub_ref = next(it), next(it)
    if mode in ("select", "window"):
        cg_ref, eg_ref = next(it), next(it)
    if mode == "select":
        sel_ref, ex_ref = next(it), next(it)
    o_ref = next(it)
    q2_ref, m_ref, acc_ref = next(it), next(it), next(it)

    p = pl.program_id(0)
    i = pl.program_id(1)
    rg = 2 * t
    reps = t // LANE
    for b in range(npi):
        q2_ref[b * rg:(b + 1) * rg] = _split_halves(q_ref[b]) * jnp.asarray(QK_SCALE, BF16)
    m_ref[...] = jnp.full(m_ref.shape, NEG, F32)
    acc_ref[...] = jnp.zeros(acc_ref.shape, F32)
    ones = jnp.ones((t, LANE), BF16)

    def step(j, ob):
        ks = pl.multiple_of(j * t, t)
        k = k_ref[pl.ds(ks, t), :]
        vaug = jnp.concatenate([v_ref[pl.ds(ks, t), :], ones], axis=1)
        colv = (lax.broadcasted_iota(jnp.int32, (1, t), 1) + (j - i) * t).astype(F32)
        if mode == "select":
            per = LANE * SLC_LEN // t
            selcat = jnp.concatenate([sel_ref[0, j // per], sel_ref[1, j // per]], axis=0)
            maskb = jnp.dot(selcat, ex_ref[j % per], preferred_element_type=F32)
        for b in range(npi):
            s = _dot_nt(q2_ref[b * rg:(b + 1) * rg], k)
            ps, alphas = [], []
            for hf in range(2):
                g0 = b * rg + hf * t
                sb = s[hf * t:(hf + 1) * t] + slope_ref[(p * npi + b) * 2 + hf] * colv
                if ob is not None:
                    sb = sb + tab_ref[ob]
                if mode == "select":
                    sb = sb + maskb[hf * t:(hf + 1) * t]
                m_prev = m_ref[g0:g0 + t]
                m_new = jnp.maximum(m_prev, jnp.max(sb, axis=1, keepdims=True))
                alphas.append(jnp.exp(m_prev - m_new))
                m_ref[g0:g0 + t] = m_new
                mb = jnp.concatenate([m_new] * reps, axis=1)
                ps.append(jnp.exp(sb - mb).astype(BF16))
            al = jnp.concatenate(alphas, axis=0)
            pv = jnp.dot(jnp.concatenate(ps, axis=0), vaug, preferred_element_type=F32)
            acc_ref[b * rg:(b + 1) * rg] = (jnp.concatenate([al, al], axis=1)
                                            * acc_ref[b * rg:(b + 1) * rg] + pv)

    j_lo = 0 if max_ob is None else jnp.maximum(i - max_ob, 0)
    j_mid = jnp.maximum(i - n_tab + 1, j_lo)
    if mode == "select":
        def body(j, c):
            @pl.when(flag_ref[i * nt + j] > 0)
            def _():
                step(j, None)
            return c

        lax.fori_loop(j_lo, j_mid, body, 0)
    else:
        def body2(c, carry):
            step(j_lo + 2 * c, None)
            step(j_lo + 2 * c + 1, None)
            return carry

        lax.fori_loop(0, jnp.right_shift(j_mid - j_lo, 1), body2, 0)

        @pl.when(jnp.bitwise_and(j_mid - j_lo, 1) == 1)
        def _():
            step(j_mid - 1, None)
    for ob in reversed(range(n_tab)):
        @pl.when(i - ob >= 0)
        def _(ob=ob):
            step(i - ob, ob)

    for b in range(npi):
        a = acc_ref[b * rg:(b + 1) * rg]
        o2 = a[:, :LANE] / a[:, LANE:]
        if mode == "diff":
            lam_init = lam_ref[4:5, 0:1]
            lam = (jnp.exp(jnp.sum(lam_ref[0:1] * lam_ref[1:2], keepdims=True))
                   - jnp.exp(jnp.sum(lam_ref[2:3] * lam_ref[3:4], keepdims=True)) + lam_init)
            o = o2[:t] - lam * o2[t:]
            o_ref[b] = (_rms(o, gsub_ref[...]) * (1.0 - lam_init)).astype(BF16)
        else:
            o = _merge_halves(o2, t)
            if mode in ("select", "window"):
                o = o * _gate_lanes(cg_ref[...], eg_ref[b])
            o_ref[b] = o.astype(BF16)


def _flash(proj, slopes, tab, *, mode, qblk, kblk, vblk, npi, n_outer, kv_per_outer,
           max_ob=None, flags=None, extra=(), extra_specs=()):
    _, s, _ = proj.shape
    n_tab, t, _ = tab.shape
    rows = 2 * npi * t
    if flags is None:
        flags = jnp.zeros((1,), jnp.int32)
    kvmap = ((lambda b: (lambda p, i, fl, sl: (b + p, 0, 0))) if kv_per_outer
             else (lambda b: (lambda p, i, fl, sl: (b, 0, 0))))
    in_specs = [pl.BlockSpec((npi, t, LANE), lambda p, i, fl, sl: (qblk // npi + p, i, 0)),
                pl.BlockSpec((None, s, LANE), kvmap(kblk)),
                pl.BlockSpec((None, s, LANE), kvmap(vblk)),
                pl.BlockSpec(tab.shape, lambda p, i, fl, sl: (0, 0, 0))] + list(extra_specs)
    grid_spec = pltpu.PrefetchScalarGridSpec(
        num_scalar_prefetch=2,
        grid=(n_outer, s // t),
        in_specs=in_specs,
        out_specs=pl.BlockSpec((npi, t, LANE), lambda p, i, fl, sl: (p, i, 0)),
        scratch_shapes=[pltpu.VMEM((rows, LANE), BF16), pltpu.VMEM((rows, LANE), F32),
                        pltpu.VMEM((rows, 2 * LANE), F32)])
    return pl.pallas_call(
        functools.partial(_flash_kernel, mode=mode, npi=npi, t=t, nt=s // t, n_tab=n_tab,
                          max_ob=max_ob),
        grid_spec=grid_spec,
        out_shape=jax.ShapeDtypeStruct((n_outer * npi, s, LANE), BF16),
        compiler_params=_cparams(("parallel", "arbitrary")),
        name="flash_" + mode,
    )(flags, slopes, proj, proj, proj, tab, *extra)


def _compress_kernel(c_ref, pos_ref, w1_ref, w2_ref, o_ref):
    c = c_ref[0].astype(F32)
    half = c.shape[1]
    pos = pos_ref[0]
    xa = (c + pos[:, :half]).astype(BF16)
    xb = (c + pos[:, half:]).astype(BF16)
    u = jnp.dot(xa, w1_ref[0, :half], preferred_element_type=F32)
    v = jnp.dot(xb, w1_ref[0, half:], preferred_element_type=F32)
    hid = u + pltpu.roll(v, shift=v.shape[0] - 1, axis=0)
    act = hid * jax.nn.sigmoid(hid)
    o_ref[0] = jnp.dot(act.astype(BF16), w2_ref[0], preferred_element_type=F32)


def _compress(cin, pos, w1, w2):
    n, nc, width = cin.shape
    return pl.pallas_call(
        _compress_kernel,
        grid=(n,),
        in_specs=[pl.BlockSpec((1, nc, width), lambda a: (a, 0, 0)),
                  pl.BlockSpec((1, 1, 2 * width), lambda a: (a // 2, 0, 0)),
                  pl.BlockSpec((1, 2 * width, LANE), lambda a: (a // 2, 0, 0)),
                  pl.BlockSpec((1, LANE, LANE), lambda a: (a // 2, 0, 0))],
        out_specs=pl.BlockSpec((1, nc, LANE), lambda a: (a, 0, 0)),
        out_shape=jax.ShapeDtypeStruct((n, nc, LANE), F32),
        compiler_params=_cparams(("parallel",)),
        name="nsa_compress",
    )(cin, pos, w1, w2)


def _cmp_select_kernel(slope_ref, q_ref, kc_ref, vc_ref, cg_ref, eg_ref, m_ref,
                       o_ref, sel_ref, any_ref, *, tq, n_cmp, n_slc, n_sel):
    i = pl.program_id(0)
    t = i * tq + lax.broadcasted_iota(jnp.int32, (tq, 1), 0)
    cmp_end = lax.broadcasted_iota(jnp.int32, (1, n_cmp), 1) * CMP_STRIDE + (CMP_LEN - 1)
    d_c = t - cmp_end
    valid = d_c >= 0
    d_cf = d_c.astype(F32)
    kc = kc_ref[...]
    vc = vc_ref[...]
    imp = [jnp.zeros((tq, n_cmp), F32), jnp.zeros((tq, n_cmp), F32)]
    for b in range(C_PAIRS):
        q2 = _split_halves(q_ref[b]) * jnp.asarray(QK_SCALE, BF16)
        s2 = _dot_nt(q2, kc)
        probs = []
        for hf in range(2):
            s = s2[hf * tq:(hf + 1) * tq] - slope_ref[2 * b + hf] * d_cf
            s = jnp.where(valid, s, NEG)
            mx = jnp.max(s, axis=1, keepdims=True)
            e = jnp.where(valid, jnp.exp(s - mx), 0.0)
            den = jnp.maximum(jnp.sum(e, axis=1, keepdims=True), 1e-30)
            pr = e / den
            imp[hf] = imp[hf] + pr
            probs.append(pr.astype(BF16))
        o2 = jnp.dot(jnp.concatenate(probs, axis=0), vc, preferred_element_type=F32)
        o = _merge_halves(o2, tq) * _gate_lanes(cg_ref[...], eg_ref[b])
        o_ref[b] = o.astype(BF16)

    imp2 = jnp.concatenate(imp, axis=0)
    slc = _dot_hilo(imp2, m_ref[...])
    t2 = jnp.concatenate([t, t], axis=0)
    blk = lax.broadcasted_iota(jnp.int32, (2 * tq, n_slc), 1)
    tb = jnp.right_shift(t2, int(math.log2(SLC_LEN)))
    forced = (blk == 0) | (blk == tb) | (blk == tb - 1)
    allowed = blk * SLC_LEN <= t2
    score = jnp.where(allowed, jnp.where(forced, FORCE_SCORE, slc), -1.0)
    picked = jnp.zeros(score.shape, jnp.int32)
    for _ in range(n_sel):
        mx = jnp.max(score, axis=1, keepdims=True)
        first = jnp.min(jnp.where(score == mx, blk, n_slc), axis=1, keepdims=True)
        hit = blk == first
        picked = jnp.where(hit, 1, picked)
        score = jnp.where(hit, -2.0, score)
    selm = jnp.where(picked > 0, 0.0, -1.0).astype(BF16)
    for g in range(C_GROUPS):
        for hf in range(n_slc // LANE):
            sel_ref[g, hf] = selm[g * tq:(g + 1) * tq, hf * LANE:(hf + 1) * LANE]
    any_ref[0] = jnp.max(jnp.where(allowed, picked, 0), axis=0, keepdims=True)


def _cmp_select(proj, kc, vc, slopes, eg, mtab):
    _, s, _ = proj.shape
    tq = min(TQ_CMP, s)
    n_cmp = kc.shape[0]
    n_slc = mtab.shape[1]
    return pl.pallas_call(
        functools.partial(_cmp_select_kernel, tq=tq, n_cmp=n_cmp, n_slc=n_slc,
                          n_sel=min(N_SELECT, s // SLC_LEN)),
        grid=(s // tq,),
        in_specs=[pl.BlockSpec(memory_space=pltpu.SMEM),
                  pl.BlockSpec((C_PAIRS, tq, LANE), lambda i: (BLK_CQ // C_PAIRS, i, 0)),
                  pl.BlockSpec((n_cmp, LANE), lambda i: (0, 0)),
                  pl.BlockSpec((n_cmp, LANE), lambda i: (0, 0)),
                  pl.BlockSpec((None, tq, LANE), lambda i: (BLK_CG, i, 0)),
                  pl.BlockSpec((C_PAIRS, LANE, LANE), lambda i: (0, 0, 0)),
                  pl.BlockSpec((n_cmp, n_slc), lambda i: (0, 0))],
        out_specs=[pl.BlockSpec((C_PAIRS, tq, LANE), lambda i: (0, i, 0)),
                   pl.BlockSpec((C_GROUPS, n_slc // LANE, tq, LANE), lambda i: (0, 0, i, 0)),
                   pl.BlockSpec((1, 1, n_slc), lambda i: (i, 0, 0))],
        out_shape=[jax.ShapeDtypeStruct((C_PAIRS, s, LANE), BF16),
                   jax.ShapeDtypeStruct((C_GROUPS, n_slc // LANE, s, LANE), BF16),
                   jax.ShapeDtypeStruct((s // tq, 1, n_slc), jnp.int32)],
        compiler_params=_cparams(("parallel",)),
        name="nsa_cmp_select",
    )(slopes, proj, kc, vc, proj, eg, mtab)


def _outproj_kernel(x_ref, oa_ref, ob_ref, oc_ref, os_ref, ow_ref, w_ref, o_ref):
    parts = [oa_ref[b] for b in range(oa_ref.shape[0])]
    parts += [ob_ref[b] for b in range(ob_ref.shape[0])]
    for b in range(oc_ref.shape[0]):
        parts.append((oc_ref[b].astype(F32) + os_ref[b].astype(F32)
                      + ow_ref[b].astype(F32)).astype(BF16))
    mix = jnp.concatenate(parts, axis=1)
    n = o_ref.shape[1]
    for c in range(n // 512):
        sl = slice(c * 512, (c + 1) * 512)
        o_ref[:, sl] = x_ref[:, sl] + jnp.dot(mix, w_ref[:, sl], preferred_element_type=F32)


def _outproj(x, oa, ob, oc, os_, ow, w):
    s, d = x.shape
    tm = min(512, s)
    blk = lambda a: pl.BlockSpec((a.shape[0], tm, LANE), lambda i: (0, i, 0))
    return pl.pallas_call(
        _outproj_kernel,
        grid=(s // tm,),
        in_specs=[pl.BlockSpec((tm, d), lambda i: (i, 0)), blk(oa), blk(ob), blk(oc), blk(os_), blk(ow),
                  pl.BlockSpec(w.shape, lambda i: (0, 0))],
        out_specs=pl.BlockSpec((tm, d), lambda i: (i, 0)),
        out_shape=jax.ShapeDtypeStruct((s, d), F32),
        compiler_params=_cparams(("parallel",)),
        name="outproj",
    )(x, oa, ob, oc, os_, ow, w)


def _ffn_kernel(te_ref, nt_ref, x_ref, g_ref, wg_ref, wu_ref, wd_ref, o_ref, h_ref, acc_ref, *, residual):
    t = pl.program_id(0)
    j = pl.program_id(1)

    @pl.when(t < nt_ref[0])
    def _():
        @pl.when(j == 0)
        def _():
            h_ref[...] = _rms(x_ref[...], g_ref[...]).astype(BF16)
            acc_ref[...] = jnp.zeros(acc_ref.shape, F32)

        h = h_ref[...]
        a = jnp.dot(h, wg_ref[0], preferred_element_type=F32)
        u = jnp.dot(h, wu_ref[0], preferred_element_type=F32)
        z = (a * jax.nn.sigmoid(a) * u).astype(BF16)
        acc_ref[...] += jnp.dot(z, wd_ref[0], preferred_element_type=F32)

        @pl.when(j == pl.num_programs(1) - 1)
        def _():
            if residual:
                o_ref[...] = x_ref[...] + acc_ref[...]
            else:
                o_ref[...] = acc_ref[...]

    @pl.when(t >= nt_ref[0])
    def _():
        o_ref[...] = jnp.zeros(o_ref.shape, F32)


def _ffn(x, g, wg, wu, wd, tile_expert, n_tiles, *, residual, tm):
    r, d = x.shape
    f = wg.shape[2]
    tf = 512 if f % 512 == 0 else f
    grid_spec = pltpu.PrefetchScalarGridSpec(
        num_scalar_prefetch=2,
        grid=(r // tm, f // tf),
        in_specs=[pl.BlockSpec((tm, d), lambda t, j, te, nt: (t, 0)),
                  pl.BlockSpec((1, d), lambda t, j, te, nt: (0, 0)),
                  pl.BlockSpec((1, d, tf), lambda t, j, te, nt: (te[t], 0, j)),
                  pl.BlockSpec((1, d, tf), lambda t, j, te, nt: (te[t], 0, j)),
                  pl.BlockSpec((1, tf, d), lambda t, j, te, nt: (te[t], j, 0))],
        out_specs=pl.BlockSpec((tm, d), lambda t, j, te, nt: (t, 0)),
        scratch_shapes=[pltpu.VMEM((tm, d), BF16), pltpu.VMEM((tm, d), F32)])
    return pl.pallas_call(
        functools.partial(_ffn_kernel, residual=residual),
        grid_spec=grid_spec,
        out_shape=jax.ShapeDtypeStruct((r, d), F32),
        compiler_params=_cparams(("parallel", "arbitrary")),
        name="ffn",
    )(tile_expert, n_tiles, x, g, wg, wu, wd)


def _router_kernel(x_ref, g_ref, w_ref, b_ref, o_ref):
    h = _rms(x_ref[...], g_ref[...])
    w = w_ref[...]
    w_hi = w.astype(BF16)
    w_lo = (w - w_hi.astype(F32)).astype(BF16)
    h_hi = h.astype(BF16)
    h_lo = (h - h_hi.astype(F32)).astype(BF16)
    logits = (jnp.dot(h_hi, w_hi, preferred_element_type=F32)
              + jnp.dot(h_lo, w_hi, preferred_element_type=F32)
              + jnp.dot(h_hi, w_lo, preferred_element_type=F32)) + b_ref[...]
    lane = lax.broadcasted_iota(jnp.int32, logits.shape, 1)
    logits = jnp.where(lane < N_EXPERTS, logits, NEG)
    l1 = jnp.max(logits, axis=1, keepdims=True)
    i1 = jnp.min(jnp.where(logits == l1, lane, LANE), axis=1, keepdims=True)
    rest = jnp.where(lane == i1, NEG, logits)
    l2 = jnp.max(rest, axis=1, keepdims=True)
    i2 = jnp.min(jnp.where(rest == l2, lane, LANE), axis=1, keepdims=True)
    e2 = jnp.exp(l2 - l1)
    w1 = 1.0 / (1.0 + e2)
    w2 = e2 / (1.0 + e2)
    out = jnp.where(lane == 0, i1.astype(F32),
                    jnp.where(lane == 1, i2.astype(F32),
                              jnp.where(lane == 2, w1, jnp.where(lane == 3, w2, 0.0))))
    o_ref[...] = out


def _router(x, g, w, b):
    s, d = x.shape
    tm = min(512, s)
    return pl.pallas_call(
        _router_kernel,
        grid=(s // tm,),
        in_specs=[pl.BlockSpec((tm, d), lambda i: (i, 0)),
                  pl.BlockSpec((1, d), lambda i: (0, 0)),
                  pl.BlockSpec((d, LANE), lambda i: (0, 0)),
                  pl.BlockSpec((1, LANE), lambda i: (0, 0))],
        out_specs=pl.BlockSpec((tm, LANE), lambda i: (i, 0)),
        out_shape=jax.ShapeDtypeStruct((s, LANE), F32),
        compiler_params=_cparams(("parallel",)),
        name="router",
    )(x, g, w, b)


def _row_copy(src_hbm, row, dst_ref, r, sem):
    return pltpu.make_async_copy(src_hbm.at[pl.ds(row, 1), :], dst_ref.at[pl.ds(r, 1), :], sem)


def _gather_kernel(idx_ref, x_hbm, o_ref, sem, *, tm):
    base = pl.program_id(0) * tm

    def issue(r, c):
        _row_copy(x_hbm, idx_ref[base + r], o_ref, r, sem).start()
        return c

    lax.fori_loop(0, tm, issue, 0)

    def wait(r, c):
        _row_copy(x_hbm, 0, o_ref, r, sem).wait()
        return c

    lax.fori_loop(0, tm, wait, 0)


def _gather_rows(x, idx, tm):
    d = x.shape[1]
    n = idx.shape[0]
    grid_spec = pltpu.PrefetchScalarGridSpec(
        num_scalar_prefetch=1,
        grid=(n // tm,),
        in_specs=[pl.BlockSpec(memory_space=pl.ANY)],
        out_specs=pl.BlockSpec((tm, d), lambda i, idx: (i, 0)),
        scratch_shapes=[pltpu.SemaphoreType.DMA(())])
    return pl.pallas_call(
        functools.partial(_gather_kernel, tm=tm),
        grid_spec=grid_spec,
        out_shape=jax.ShapeDtypeStruct((n, d), x.dtype),
        compiler_params=_cparams(("arbitrary",)),
        name="moe_gather",
    )(idx, x)


def _combine_kernel(d0_ref, d1_ref, x_ref, r_ref, y_hbm, o_ref, b0_ref, b1_ref, sem, *, tm):
    base = pl.program_id(0) * tm

    def issue(r, c):
        _row_copy(y_hbm, d0_ref[base + r], b0_ref, r, sem).start()
        _row_copy(y_hbm, d1_ref[base + r], b1_ref, r, sem).start()
        return c

    lax.fori_loop(0, tm, issue, 0)

    def wait(r, c):
        _row_copy(y_hbm, 0, b0_ref, r, sem).wait()
        _row_copy(y_hbm, 0, b1_ref, r, sem).wait()
        return c

    lax.fori_loop(0, tm, wait, 0)
    rt = r_ref[...]
    o_ref[...] = x_ref[...] + rt[:, 2:3] * b0_ref[...] + rt[:, 3:4] * b1_ref[...]


def _combine(x, route, y, d0, d1):
    s, d = x.shape
    tm = min(256, s)
    grid_spec = pltpu.PrefetchScalarGridSpec(
        num_scalar_prefetch=2,
        grid=(s // tm,),
        in_specs=[pl.BlockSpec((tm, d), lambda i, a, b: (i, 0)),
                  pl.BlockSpec((tm, LANE), lambda i, a, b: (i, 0)),
                  pl.BlockSpec(memory_space=pl.ANY)],
        out_specs=pl.BlockSpec((tm, d), lambda i, a, b: (i, 0)),
        scratch_shapes=[pltpu.VMEM((tm, d), F32), pltpu.VMEM((tm, d), F32),
                        pltpu.SemaphoreType.DMA(())])
    return pl.pallas_call(
        functools.partial(_combine_kernel, tm=tm),
        grid_spec=grid_spec,
        out_shape=jax.ShapeDtypeStruct((s, d), F32),
        compiler_params=_cparams(("arbitrary",)),
        name="moe_combine",
    )(d0, d1, x, route, y)


def _moe(x, g, w_router, b_router, wg, wu, wd):
    s, d = x.shape
    tm = min(512, s)
    wr = jnp.zeros((d, LANE), F32).at[:, :N_EXPERTS].set(w_router)
    br = jnp.zeros((1, LANE), F32).at[0, :N_EXPERTS].set(b_router)
    route = _router(x, g, wr, br)
    e = route[:, :2].astype(jnp.int32).reshape(-1)
    onehot = (e[:, None] == jnp.arange(N_EXPERTS)[None, :]).astype(jnp.int32)
    rank = jnp.sum((jnp.cumsum(onehot, axis=0) - onehot) * onehot, axis=1)
    counts = jnp.sum(onehot, axis=0)
    padded = ((counts + tm - 1) // tm) * tm
    ends = jnp.cumsum(padded)
    dest = (ends - padded)[e] + rank
    n_rows = 2 * s + N_EXPERTS * tm
    row_token = jnp.zeros((n_rows,), jnp.int32).at[dest].set(jnp.arange(2 * s, dtype=jnp.int32) // 2)
    tile_start = jnp.arange(n_rows // tm, dtype=jnp.int32) * tm
    tile_expert = jnp.minimum(jnp.sum(tile_start[:, None] >= ends[None, :], axis=1), N_EXPERTS - 1)
    n_tiles = (ends[-1] // tm).astype(jnp.int32).reshape(1)
    xs = _gather_rows(x, row_token, tm)
    ys = _ffn(xs, g, wg, wu, wd, tile_expert.astype(jnp.int32), n_tiles, residual=False, tm=tm)
    dest2 = dest.reshape(s, 2).astype(jnp.int32)
    return _combine(x, route, ys, dest2[:, 0], dest2[:, 1])


def _norm_kernel(x_ref, g_ref, o_ref):
    o_ref[...] = _rms(x_ref[...], g_ref[...])


def _final_norm(x, g):
    s, d = x.shape
    tm = min(512, s)
    return pl.pallas_call(
        _norm_kernel,
        grid=(s // tm,),
        in_specs=[pl.BlockSpec((tm, d), lambda i: (i, 0)), pl.BlockSpec((1, d), lambda i: (0, 0))],
        out_specs=pl.BlockSpec((tm, d), lambda i: (i, 0)),
        out_shape=jax.ShapeDtypeStruct((s, d), F32),
        compiler_params=_cparams(("parallel",)),
        name="final_norm",
    )(x, g)


def kernel(x, ln_attn, w_in, w_out, lam_q1, lam_k1, lam_q2, lam_k2, subln,
           cmp_pos_k, cmp_w1_k, cmp_w2_k, cmp_pos_v, cmp_w1_v, cmp_w2_v,
           ln_ffn, ffn_w_gate, ffn_w_up, ffn_w_down,
           router_w, router_b, exp_w_gate, exp_w_up, exp_w_down, ln_final):
    bsz, s, d = x.shape
    assert bsz == 1 and s % T_BIG == 0
    depth = w_in.shape[0]
    tb, tw = T_BIG, T_WIN
    n_cmp = s // CMP_STRIDE
    nt = s // tb

    in_cols = _in_col_perm(d)
    in_valid = jnp.asarray((in_cols >= 0).astype(np.float32))
    in_cols = jnp.asarray(np.maximum(in_cols, 0))
    out_rows = jnp.asarray(_out_row_perm())

    slopes_a = jnp.asarray(np.repeat(_alibi(A_HEADS), 2))
    slopes_b = jnp.asarray(_alibi(B_HEADS))
    sl_c = _alibi(C_HEADS)
    slopes_c = jnp.asarray(np.stack([sl_c[:C_PAIRS], sl_c[C_PAIRS:]], axis=1).reshape(-1))
    tab_causal = jnp.asarray(_causal_table(tb))
    tab_dil = jnp.asarray(_dilated_table(tb))
    tab_win = jnp.asarray(_window_table(tw))
    ex_sel = jnp.asarray(_select_expand_table(tb), BF16)
    eg = jnp.asarray(_gate_expand_table(), BF16)
    n_slc_pad = -(-(s // SLC_LEN) // LANE) * LANE
    mtab = jnp.asarray(_cmp_to_slc_table(n_cmp, n_slc_pad), BF16)
    one_tile = jnp.zeros((s // min(512, s),), jnp.int32)

    def cg_spec(t):
        return pl.BlockSpec((None, t, LANE), lambda p, i, fl, sl: (BLK_CG, i, 0))

    eg_spec = pl.BlockSpec((C_PAIRS, LANE, LANE), lambda p, i, fl, sl: (0, 0, 0))

    xs = x[0]
    for l in range(depth):
        w = (jnp.take(w_in[l], in_cols, axis=1) * in_valid[None, :]).astype(BF16)
        proj = _inproj(xs, ln_attn[l][None, :], w)

        lam_init = 0.8 - 0.6 * math.exp(-0.3 * l)
        lam_tab = jnp.zeros((8, LANE), F32).at[:4, :HEAD_DIM].set(
            jnp.stack([lam_q1[l], lam_k1[l], lam_q2[l], lam_k2[l]])).at[4, :].set(lam_init)
        oa = _flash(proj, slopes_a, tab_causal, mode="diff", qblk=BLK_AQ, kblk=BLK_AK, vblk=BLK_AV,
                    npi=1, n_outer=A_HEADS, kv_per_outer=True,
                    extra=(lam_tab, subln[l][None, :]),
                    extra_specs=(pl.BlockSpec((8, LANE), lambda p, i, fl, sl: (0, 0)),
                                 pl.BlockSpec((1, LANE), lambda p, i, fl, sl: (0, 0))))

        ob = _flash(proj, slopes_b, tab_dil, mode="pair", qblk=BLK_BQ, kblk=BLK_BK, vblk=BLK_BV,
                    npi=1, n_outer=B_HEADS // 2, kv_per_outer=True, max_ob=tab_dil.shape[0] - 1)

        pc = proj[BLK_CKC:BLK_CVC + 1].reshape(2, n_cmp, CMP_STRIDE, C_GROUPS, HEAD_DIM)
        cin = pc.transpose(0, 3, 1, 2, 4).reshape(2 * C_GROUPS, n_cmp, CMP_STRIDE * HEAD_DIM)
        pos = jnp.stack([cmp_pos_k[l], cmp_pos_v[l]]).reshape(2, 1, CMP_LEN * HEAD_DIM)
        w1 = jnp.stack([cmp_w1_k[l], cmp_w1_v[l]]).astype(BF16)
        w2 = jnp.zeros((2, LANE, LANE), F32).at[:, :, :HEAD_DIM].set(
            jnp.stack([cmp_w2_k[l], cmp_w2_v[l]])).astype(BF16)
        cmp = _compress(cin, pos, w1, w2)
        kc = jnp.concatenate([cmp[0, :, :HEAD_DIM], cmp[1, :, :HEAD_DIM]], axis=1).astype(BF16)
        vc = jnp.concatenate([cmp[2, :, :HEAD_DIM], cmp[3, :, :HEAD_DIM]], axis=1).astype(BF16)
        oc, selm, anyblk = _cmp_select(proj, kc, vc, slopes_c, eg[0], mtab)
        flags = anyblk[:, 0, :s // SLC_LEN].reshape(nt, -1, nt, tb // SLC_LEN).max(axis=(1, 3)).reshape(-1)
        os_ = _flash(proj, slopes_c, tab_causal, mode="select", qblk=BLK_CQ, kblk=BLK_CKS, vblk=BLK_CVS,
                     npi=C_PAIRS, n_outer=1, kv_per_outer=False, flags=flags,
                     extra=(proj, eg[1], selm, ex_sel),
                     extra_specs=(cg_spec(tb), eg_spec,
                                  pl.BlockSpec((C_GROUPS, selm.shape[1], tb, LANE),
                                               lambda p, i, fl, sl: (0, 0, i, 0)),
                                  pl.BlockSpec(ex_sel.shape, lambda p, i, fl, sl: (0, 0, 0))))
        ow = _flash(proj, slopes_c, tab_win, mode="window", qblk=BLK_CQ, kblk=BLK_CKW, vblk=BLK_CVW,
                    npi=C_PAIRS, n_outer=1, kv_per_outer=False, max_ob=tab_win.shape[0] - 1,
                    extra=(proj, eg[2]), extra_specs=(cg_spec(tw), eg_spec))

        xs = _outproj(xs, oa, ob, oc, os_, ow, jnp.take(w_out[l], out_rows, axis=0).astype(BF16))

        j = l // 2
        g_ffn = ln_ffn[l][None, :]
        if l % 2 == 0:
            n_t = jnp.full((1,), one_tile.shape[0], jnp.int32)
            xs = _ffn(xs, g_ffn, ffn_w_gate[j][None].astype(BF16), ffn_w_up[j][None].astype(BF16),
                      ffn_w_down[j][None].astype(BF16), one_tile, n_t, residual=True, tm=min(512, s))
        else:
            xs = _moe(xs, g_ffn, router_w[j], router_b[j], exp_w_gate[j].astype(BF16),
                      exp_w_up[j].astype(BF16), exp_w_down[j].astype(BF16))
    return _final_norm(xs, ln_final[None, :])[None]
```

```python
import functools
import math

import numpy as np
import jax
import jax.numpy as jnp
from jax import lax
from jax.experimental import pallas as pl
from jax.experimental.pallas import tpu as pltpu

F32 = jnp.float32
BF16 = jnp.bfloat16

HEAD_DIM = 64
LANE = 128
A_HEADS = 4
B_HEADS = 8
C_HEADS = 16
C_GROUPS = 2
C_PAIRS = C_HEADS // C_GROUPS
N_BRANCH = 3
CMP_LEN = 32
CMP_STRIDE = 16
SLC_LEN = 64
N_SELECT = 16
WIN = 512
DILATED_PATTERNS = ((128, 1), (512, 4), (2048, 16))
N_EXPERTS = 8
EPS = 1e-6
NEG = -1e30
FORCE_SCORE = 1e6
QK_SCALE = HEAD_DIM ** -0.5
LOG2E = math.log2(math.e)

BLK_AQ, BLK_AK, BLK_AV = 0, 4, 8
BLK_BQ, BLK_BK, BLK_BV = 12, 16, 20
BLK_CQ = 24
BLK_CKC, BLK_CVC, BLK_CKS, BLK_CVS, BLK_CKW, BLK_CVW, BLK_CG = 32, 33, 34, 35, 36, 37, 38
N_BLK = 40

T_BIG = 512
T_WIN = 256
TQ_CMP = 128
TILE_UNROLL = 4
VMEM_LIMIT = 56 * 1024 * 1024


def _cparams(sem):
    return pltpu.CompilerParams(dimension_semantics=sem, vmem_limit_bytes=VMEM_LIMIT)


def _in_col_perm(d_mix):
    a_w = A_HEADS * HEAD_DIM
    sizes = [a_w, a_w, a_w, a_w, A_HEADS * 2 * HEAD_DIM,
             B_HEADS * HEAD_DIM, B_HEADS * HEAD_DIM, B_HEADS * HEAD_DIM,
             C_HEADS * HEAD_DIM] + [C_GROUPS * HEAD_DIM] * 6 + [C_HEADS * N_BRANCH]
    off = np.concatenate([[0], np.cumsum(sizes)])
    (aq1, aq2, ak1, ak2, av, bq, bk, bv, cq, ckc, cvc, cks, cvs, ckw, cvw, cg) = off[:-1]
    r = np.arange
    cols = []
    for h in range(A_HEADS):
        cols += [aq1 + h * 64 + r(64), aq2 + h * 64 + r(64)]
    for h in range(A_HEADS):
        cols += [ak1 + h * 64 + r(64), ak2 + h * 64 + r(64)]
    for h in range(A_HEADS):
        cols += [av + h * 128 + r(128)]
    for base in (bq, bk, bv):
        cols += [base + r(B_HEADS * HEAD_DIM)]
    for j in range(C_PAIRS):
        cols += [cq + j * 64 + r(64), cq + (C_PAIRS + j) * 64 + r(64)]
    for base in (ckc, cvc, cks, cvs, ckw, cvw):
        cols += [base + r(128)]
    cols += [cg + r(C_HEADS * N_BRANCH), -np.ones(LANE - C_HEADS * N_BRANCH, np.int64)]
    cols += [-np.ones(LANE, np.int64)]
    cols = np.concatenate(cols)
    assert cols.shape[0] == N_BLK * LANE and off[-1] == cols.max() + 1
    return cols


def _out_row_perm():
    rows = [np.arange(A_HEADS * 2 * HEAD_DIM + B_HEADS * HEAD_DIM)]
    c0 = rows[0].shape[0]
    for j in range(C_PAIRS):
        rows += [c0 + j * 64 + np.arange(64), c0 + (C_PAIRS + j) * 64 + np.arange(64)]
    return np.concatenate(rows)


def _alibi(n):
    return np.exp2(-8.0 * np.arange(1, n + 1, dtype=np.float64) / n).astype(np.float32)


def _causal_table(t):
    d = np.arange(t)[:, None] - np.arange(t)[None, :]
    return np.where(d >= 0, 0.0, NEG).astype(np.float32)[None]


def _dilated_table(t):
    n_ob = max(w for w, _ in DILATED_PATTERNS) // t + 1
    tabs = []
    for ob in range(n_ob):
        d = ob * t + np.arange(t)[:, None] - np.arange(t)[None, :]
        c = np.zeros(d.shape, np.float64)
        for w, dil in DILATED_PATTERNS:
            c += (d >= 0) & (d <= w) & (d % dil == 0)
        tabs.append(np.where(c > 0, np.log2(np.maximum(c, 1.0)), NEG))
    return np.stack(tabs).astype(np.float32)


def _window_table(t):
    n_ob = (WIN - 1 + t - 1) // t + 1
    tabs = []
    for ob in range(n_ob):
        d = ob * t + np.arange(t)[:, None] - np.arange(t)[None, :]
        tabs.append(np.where((d >= 0) & (d < WIN), 0.0, NEG))
    return np.stack(tabs).astype(np.float32)


def _select_expand_table(t):
    per = t // SLC_LEN
    n = LANE // per
    e = np.zeros((n, t, LANE), np.float32)
    for jj in range(n):
        e[jj, np.arange(t), jj * per + np.arange(t) // SLC_LEN] = 1e30
    return e


def _split3(v):
    v = np.asarray(v, np.float32)
    hi = v.astype(BF16)
    r = v - hi.astype(np.float32)
    mid = r.astype(BF16)
    lo = (r - mid.astype(np.float32)).astype(BF16)
    return hi, mid, lo


def _alibi_key_lanes(slopes2, cols):
    n_pairs = slopes2.shape[0] // 2
    v = (slopes2.astype(np.float32).reshape(n_pairs, 2, 1)
         * cols.astype(np.float32).reshape(1, 1, -1))
    parts = np.stack(_split3(v), axis=-1)
    lanes = parts.transpose(0, 2, 1, 3).reshape(n_pairs, -1, 6)
    return np.pad(lanes, ((0, 0), (0, 0), (0, LANE - 6)))


def _alibi_query_lanes(t):
    e = np.zeros((2 * t, LANE), np.float32)
    e[:t, 0:3] = 1.0
    e[t:, 3:6] = 1.0
    return e


def _gate_expand_table():
    e = np.zeros((N_BRANCH, C_PAIRS, LANE, LANE), np.float32)
    for br in range(N_BRANCH):
        for j in range(C_PAIRS):
            e[br, j, j * N_BRANCH + br, :64] = 1.0
            e[br, j, (C_PAIRS + j) * N_BRANCH + br, 64:] = 1.0
    return e


def _cmp_to_slc_table(n_cmp, n_slc):
    r = SLC_LEN // CMP_STRIDE
    m = np.zeros((n_cmp, n_slc), np.float32)
    for b in range(n_slc):
        for n in range(r * b - 1, r * b + r):
            if 0 <= n < n_cmp:
                m[n, b] = 1.0
    return m


def _rms(x, g):
    return x * lax.rsqrt(jnp.mean(x * x, axis=-1, keepdims=True) + EPS) * g


def _split_halves(q):
    lane = lax.broadcasted_iota(jnp.int32, q.shape, 1)
    zero = jnp.zeros_like(q)
    return jnp.concatenate([jnp.where(lane < HEAD_DIM, q, zero),
                            jnp.where(lane >= HEAD_DIM, q, zero)], axis=0)


def _merge_halves(o2, t):
    lane = lax.broadcasted_iota(jnp.int32, (t, LANE), 1)
    return jnp.where(lane < HEAD_DIM, o2[:t], o2[t:])


def _dot_nt(a, b):
    return lax.dot_general(a, b, (((1,), (1,)), ((), ())), preferred_element_type=F32)


def _dot_hilo(x, w):
    hi = x.astype(BF16)
    lo = (x - hi.astype(F32)).astype(BF16)
    return (jnp.dot(hi, w, preferred_element_type=F32)
            + jnp.dot(lo, w, preferred_element_type=F32))


def _gate_lanes(cg, eg):
    return _dot_hilo(jax.nn.sigmoid(cg.astype(F32)), eg)


def _inproj_kernel(x_ref, g_ref, w_ref, o_ref, h_ref, *, nb):
    @pl.when(pl.program_id(1) == 0)
    def _():
        h_ref[...] = _rms(x_ref[...], g_ref[...]).astype(BF16)

    h = h_ref[...]
    for c in range(nb // 2):
        r = jnp.dot(h, w_ref[:, c * 256:(c + 1) * 256], preferred_element_type=F32)
        o_ref[2 * c] = r[:, :LANE].astype(BF16)
        o_ref[2 * c + 1] = r[:, LANE:].astype(BF16)


def _inproj(x, g, w):
    s, d = x.shape
    tm, nb = 1024, 10
    tm = min(tm, s)
    return pl.pallas_call(
        functools.partial(_inproj_kernel, nb=nb),
        grid=(s // tm, N_BLK // nb),
        in_specs=[pl.BlockSpec((tm, d), lambda i, j: (i, 0)),
                  pl.BlockSpec((1, d), lambda i, j: (0, 0)),
                  pl.BlockSpec((d, nb * LANE), lambda i, j: (0, j))],
        out_specs=pl.BlockSpec((nb, tm, LANE), lambda i, j: (j, i, 0)),
        out_shape=jax.ShapeDtypeStruct((N_BLK, s, LANE), BF16),
        scratch_shapes=[pltpu.VMEM((tm, d), BF16)],
        compiler_params=_cparams(("parallel", "arbitrary")),
        name="inproj",
    )(x, g, w)


def _flash_kernel(*refs, mode, npi, t, nt, n_tab, max_ob):
    it = iter(refs)
    flag_ref, slope_ref = next(it), next(it)
    q_ref, k_ref, v_ref, tab_ref = next(it), next(it), next(it), next(it)
    if mode == "diff":
        lam_ref, gsub_ref = next(it), next(it)
    if mode in ("diff", "pair"):
        cb_ref, ind_ref = next(it), next(it)
    if mode in ("select", "window"):
        cg_ref, eg_ref = next(it), next(it)
    if mode == "select":
        sel_ref, ext_ref = next(it), next(it)
    o_ref = next(it)
    q2_ref, m_ref, acc_ref = next(it), next(it), next(it)

    alibi_in_matmul = mode in ("diff", "pair")
    p = pl.program_id(0)
    i = pl.program_id(1)
    rg = 2 * t
    reps = t // LANE
    for b in range(npi):
        q2 = _split_halves(q_ref[b])
        if alibi_in_matmul:
            q2 = jnp.concatenate([q2, ind_ref[...]], axis=1)
        q2_ref[b * rg:(b + 1) * rg] = q2
    m_ref[...] = jnp.full(m_ref.shape, NEG, F32)
    acc_ref[...] = jnp.zeros(acc_ref.shape, F32)
    ones = jnp.ones((t, LANE), BF16)

    def step(j, ob):
        ks = pl.multiple_of(j * t, t)
        k = k_ref[pl.ds(ks, t), :]
        vaug = jnp.concatenate([v_ref[pl.ds(ks, t), :], ones], axis=1)
        tile_off = ((j - i) * t).astype(F32)
        colv = lax.broadcasted_iota(jnp.int32, (1, t), 1).astype(F32) + tile_off
        if mode == "select":
            per = LANE * SLC_LEN // t
            k = jnp.concatenate([k, ext_ref[j % per]], axis=1)
            selcat = jnp.concatenate([sel_ref[0, j // per], sel_ref[1, j // per]], axis=0)
        for b in range(npi):
            qa = q2_ref[b * rg:(b + 1) * rg]
            if mode == "select":
                qa = jnp.concatenate([qa, selcat], axis=1)
            ka = jnp.concatenate([k, cb_ref[b]], axis=1) if alibi_in_matmul else k
            s = _dot_nt(qa, ka)
            ps, alphas = [], []
            for hf in range(2):
                g0 = b * rg + hf * t
                slope = slope_ref[(p * npi + b) * 2 + hf]
                sb = s[hf * t:(hf + 1) * t]
                if not alibi_in_matmul:
                    sb = sb + slope * colv
                if ob is not None:
                    sb = sb + tab_ref[ob]
                m_prev = m_ref[g0:g0 + t]
                row_max = jnp.max(sb, axis=1, keepdims=True)
                if alibi_in_matmul:
                    off = slope * tile_off
                    m_new = jnp.maximum(m_prev, row_max + off)
                    m_loc = m_new - off
                else:
                    m_new = jnp.maximum(m_prev, row_max)
                    m_loc = m_new
                alphas.append(jnp.exp2(m_prev - m_new))
                m_ref[g0:g0 + t] = m_new
                mb = jnp.concatenate([m_loc] * reps, axis=1)
                ps.append(jnp.exp2(sb - mb).astype(BF16))
            al = jnp.concatenate(alphas, axis=0)
            pv = jnp.dot(jnp.concatenate(ps, axis=0), vaug, preferred_element_type=F32)
            acc_ref[b * rg:(b + 1) * rg] = (jnp.concatenate([al, al], axis=1)
                                            * acc_ref[b * rg:(b + 1) * rg] + pv)

    j_lo = 0 if max_ob is None else jnp.maximum(i - max_ob, 0)
    j_mid = jnp.maximum(i - n_tab + 1, j_lo)
    if mode == "select":
        def body(j, c):
            @pl.when(flag_ref[i * nt + j] > 0)
            def _():
                step(j, None)
            return c

        lax.fori_loop(j_lo, j_mid, body, 0)
    elif max_ob is None or max_ob >= n_tab:
        n_full = jnp.right_shift(j_mid - j_lo, TILE_UNROLL.bit_length() - 1)

        def body_many(c, carry):
            for u in range(TILE_UNROLL):
                step(j_lo + TILE_UNROLL * c + u, None)
            return carry

        lax.fori_loop(0, n_full, body_many, 0)
        done = j_lo + TILE_UNROLL * n_full
        width = TILE_UNROLL // 2
        while width >= 1:
            @pl.when(jnp.bitwise_and(j_mid - j_lo, width) != 0)
            def _(done=done, width=width):
                for u in range(width):
                    step(done + u, None)

            done = done + jnp.bitwise_and(j_mid - j_lo, width)
            width //= 2
    if npi == 1 and n_tab > 1:
        @pl.when(i >= n_tab - 1)
        def _():
            for ob in reversed(range(n_tab)):
                step(i - ob, ob)
        early = n_tab - 1
    else:
        early = n_tab
    for ob in reversed(range(early)):
        @pl.when((i - ob >= 0) if early == n_tab else ((i - ob >= 0) & (i < n_tab - 1)))
        def _(ob=ob):
            step(i - ob, ob)

    for b in range(npi):
        a = acc_ref[b * rg:(b + 1) * rg]
        o2 = a[:, :LANE] / a[:, LANE:]
        if mode == "diff":
            lam_init = lam_ref[4:5, 0:1]
            lam = (jnp.exp(jnp.sum(lam_ref[0:1] * lam_ref[1:2], keepdims=True))
                   - jnp.exp(jnp.sum(lam_ref[2:3] * lam_ref[3:4], keepdims=True)) + lam_init)
            o = o2[:t] - lam * o2[t:]
            o_ref[b] = (_rms(o, gsub_ref[...]) * (1.0 - lam_init)).astype(BF16)
        else:
            o = _merge_halves(o2, t)
            if mode in ("select", "window"):
                o = o * _gate_lanes(cg_ref[...], eg_ref[b])
            o_ref[b] = o.astype(BF16)


def _flash(proj, slopes, tab, *, mode, qblk, kblk, vblk, npi, n_outer, kv_per_outer,
           max_ob=None, flags=None, extra=(), extra_specs=()):
    _, s, _ = proj.shape
    n_tab, t, _ = tab.shape
    rows = 2 * npi * t
    q_lanes = 2 * LANE if mode in ("diff", "pair") else LANE
    if flags is None:
        flags = jnp.zeros((1,), jnp.int32)
    kvmap = ((lambda b: (lambda p, i, fl, sl: (b + p, 0, 0))) if kv_per_outer
             else (lambda b: (lambda p, i, fl, sl: (b, 0, 0))))
    in_specs = [pl.BlockSpec((npi, t, LANE), lambda p, i, fl, sl: (qblk // npi + p, i, 0)),
                pl.BlockSpec((None, s, LANE), kvmap(kblk)),
                pl.BlockSpec((None, s, LANE), kvmap(vblk)),
                pl.BlockSpec(tab.shape, lambda p, i, fl, sl: (0, 0, 0))] + list(extra_specs)
    grid_spec = pltpu.PrefetchScalarGridSpec(
        num_scalar_prefetch=2,
        grid=(n_outer, s // t),
        in_specs=in_specs,
        out_specs=pl.BlockSpec((npi, t, LANE), lambda p, i, fl, sl: (p, i, 0)),
        scratch_shapes=[pltpu.VMEM((rows, q_lanes), BF16), pltpu.VMEM((rows, LANE), F32),
                        pltpu.VMEM((rows, 2 * LANE), F32)])
    return pl.pallas_call(
        functools.partial(_flash_kernel, mode=mode, npi=npi, t=t, nt=s // t, n_tab=n_tab,
                          max_ob=max_ob),
        grid_spec=grid_spec,
        out_shape=jax.ShapeDtypeStruct((n_outer * npi, s, LANE), BF16),
        compiler_params=_cparams(("parallel", "arbitrary")),
        name="flash_" + mode,
    )(flags, slopes, proj, proj, proj, tab, *extra)


def _compress_kernel(c_ref, pos_ref, w1_ref, w2_ref, o_ref):
    c = c_ref[0].astype(F32)
    half = c.shape[1]
    pos = pos_ref[0]
    xa = (c + pos[:, :half]).astype(BF16)
    xb = (c + pos[:, half:]).astype(BF16)
    u = jnp.dot(xa, w1_ref[0, :half], preferred_element_type=F32)
    v = jnp.dot(xb, w1_ref[0, half:], preferred_element_type=F32)
    hid = u + pltpu.roll(v, shift=v.shape[0] - 1, axis=0)
    act = hid * jax.nn.sigmoid(hid)
    o_ref[0] = jnp.dot(act.astype(BF16), w2_ref[0], preferred_element_type=F32)


def _compress(cin, pos, w1, w2):
    n, nc, width = cin.shape
    return pl.pallas_call(
        _compress_kernel,
        grid=(n,),
        in_specs=[pl.BlockSpec((1, nc, width), lambda a: (a, 0, 0)),
                  pl.BlockSpec((1, 1, 2 * width), lambda a: (a // 2, 0, 0)),
                  pl.BlockSpec((1, 2 * width, LANE), lambda a: (a // 2, 0, 0)),
                  pl.BlockSpec((1, LANE, LANE), lambda a: (a // 2, 0, 0))],
        out_specs=pl.BlockSpec((1, nc, LANE), lambda a: (a, 0, 0)),
        out_shape=jax.ShapeDtypeStruct((n, nc, LANE), F32),
        compiler_params=_cparams(("parallel",)),
        name="nsa_compress",
    )(cin, pos, w1, w2)


def _cmp_select_kernel(q_ref, kc_ref, vc_ref, ind_ref, cg_ref, eg_ref, m_ref,
                       o_ref, sel_ref, any_ref, *, tq, n_cmp, n_slc, n_sel):
    i = pl.program_id(0)
    t = i * tq + lax.broadcasted_iota(jnp.int32, (tq, 1), 0)
    cmp_end = lax.broadcasted_iota(jnp.int32, (1, n_cmp), 1) * CMP_STRIDE + (CMP_LEN - 1)
    maskb = jnp.where(t >= cmp_end, 0.0, NEG)
    row_ok = jnp.where(t >= CMP_LEN - 1, 1.0, 0.0)
    kc = kc_ref[...]
    vc = vc_ref[...]
    reps = n_cmp // LANE
    imp = [jnp.zeros((tq, n_cmp), F32), jnp.zeros((tq, n_cmp), F32)]
    for b in range(C_PAIRS):
        qa = jnp.concatenate([_split_halves(q_ref[b]), ind_ref[b]], axis=1)
        s2 = _dot_nt(qa, kc)
        es = []
        for hf in range(2):
            s = s2[hf * tq:(hf + 1) * tq] + maskb
            es.append(jnp.exp2(s - jnp.max(s, axis=1, keepdims=True)))
        o2 = jnp.dot(jnp.concatenate(es, axis=0).astype(BF16), vc, preferred_element_type=F32)
        ok2 = jnp.concatenate([row_ok, row_ok], axis=0)
        rden = ok2 / o2[:, LANE:]
        for hf in range(2):
            r = rden[hf * tq:(hf + 1) * tq]
            imp[hf] = imp[hf] + es[hf] * jnp.concatenate([r] * reps, axis=1)
        o = _merge_halves(o2[:, :LANE] * rden, tq) * _gate_lanes(cg_ref[...], eg_ref[b])
        o_ref[b] = o.astype(BF16)

    imp2 = jnp.concatenate(imp, axis=0)
    imp_hi = imp2.astype(BF16)
    imp_lo = (imp2 - imp_hi.astype(F32)).astype(BF16)
    slc = _dot_nt(m_ref[...], imp_hi) + _dot_nt(m_ref[...], imp_lo)
    t_row = i * tq + lax.broadcasted_iota(jnp.int32, (1, tq), 1)
    t2 = jnp.concatenate([t_row, t_row], axis=1)
    blk = lax.broadcasted_iota(jnp.int32, (n_slc, 2 * tq), 0)
    tb = jnp.right_shift(t2, int(math.log2(SLC_LEN)))
    forced = (blk == 0) | (blk == tb) | (blk == tb - 1)
    allowed = blk * SLC_LEN <= t2
    score = jnp.where(forced, -2.0, jnp.where(allowed, slc, -1.0))
    picked = jnp.where(forced, 1, 0)
    for _ in range(max(n_sel - 3, 0)):
        mx = jnp.max(score, axis=0, keepdims=True)
        first = jnp.min(jnp.where(score == mx, blk, n_slc), axis=0, keepdims=True)
        hit = blk == first
        picked = jnp.where(hit, 1, picked)
        score = jnp.where(hit, -2.0, score)
    sel_t = jnp.where(picked > 0, 0.0, -1.0)
    any_t = jnp.where(allowed & (picked > 0), 1.0, 0.0)
    both_t = jnp.concatenate([sel_t, any_t], axis=0).astype(BF16)
    rr = lax.broadcasted_iota(jnp.int32, (2 * tq, 2 * tq), 0)
    cc = lax.broadcasted_iota(jnp.int32, (2 * tq, 2 * tq), 1)
    eye = jnp.where(rr == cc, 1.0, 0.0).astype(BF16)
    both = _dot_nt(eye, both_t)
    selm = both[:, :n_slc].astype(BF16)
    for g in range(C_GROUPS):
        for hf in range(n_slc // LANE):
            sel_ref[g, hf] = selm[g * tq:(g + 1) * tq, hf * LANE:(hf + 1) * LANE]
    any_ref[0] = jnp.max(both[:, n_slc:], axis=0, keepdims=True).astype(jnp.int32)


def _cmp_select(proj, kc, vc, ind, eg, mtab):
    _, s, _ = proj.shape
    tq = ind.shape[1] // 2
    n_cmp = kc.shape[0]
    n_slc = mtab.shape[0]
    return pl.pallas_call(
        functools.partial(_cmp_select_kernel, tq=tq, n_cmp=n_cmp, n_slc=n_slc,
                          n_sel=min(N_SELECT, s // SLC_LEN)),
        grid=(s // tq,),
        in_specs=[pl.BlockSpec((C_PAIRS, tq, LANE), lambda i: (BLK_CQ // C_PAIRS, i, 0)),
                  pl.BlockSpec((n_cmp, 2 * LANE), lambda i: (0, 0)),
                  pl.BlockSpec((n_cmp, 2 * LANE), lambda i: (0, 0)),
                  pl.BlockSpec(ind.shape, lambda i: (0, 0, 0)),
                  pl.BlockSpec((None, tq, LANE), lambda i: (BLK_CG, i, 0)),
                  pl.BlockSpec((C_PAIRS, LANE, LANE), lambda i: (0, 0, 0)),
                  pl.BlockSpec((n_slc, n_cmp), lambda i: (0, 0))],
        out_specs=[pl.BlockSpec((C_PAIRS, tq, LANE), lambda i: (0, i, 0)),
                   pl.BlockSpec((C_GROUPS, n_slc // LANE, tq, LANE), lambda i: (0, 0, i, 0)),
                   pl.BlockSpec((1, 1, n_slc), lambda i: (i, 0, 0))],
        out_shape=[jax.ShapeDtypeStruct((C_PAIRS, s, LANE), BF16),
                   jax.ShapeDtypeStruct((C_GROUPS, n_slc // LANE, s, LANE), BF16),
                   jax.ShapeDtypeStruct((s // tq, 1, n_slc), jnp.int32)],
        compiler_params=_cparams(("parallel",)),
        name="nsa_cmp_select",
    )(proj, kc, vc, ind, proj, eg, mtab)


def _outproj_kernel(x_ref, oa_ref, ob_ref, oc_ref, os_ref, ow_ref, w_ref, o_ref):
    parts = [oa_ref[b] for b in range(oa_ref.shape[0])]
    parts += [ob_ref[b] for b in range(ob_ref.shape[0])]
    for b in range(oc_ref.shape[0]):
        parts.append((oc_ref[b].astype(F32) + os_ref[b].astype(F32)
                      + ow_ref[b].astype(F32)).astype(BF16))
    mix = jnp.concatenate(parts, axis=1)
    n = o_ref.shape[1]
    for c in range(n // 512):
        sl = slice(c * 512, (c + 1) * 512)
        o_ref[:, sl] = x_ref[:, sl] + jnp.dot(mix, w_ref[:, sl], preferred_element_type=F32)


def _outproj(x, oa, ob, oc, os_, ow, w):
    s, d = x.shape
    tm = min(512, s)
    blk = lambda a: pl.BlockSpec((a.shape[0], tm, LANE), lambda i: (0, i, 0))
    return pl.pallas_call(
        _outproj_kernel,
        grid=(s // tm,),
        in_specs=[pl.BlockSpec((tm, d), lambda i: (i, 0)), blk(oa), blk(ob), blk(oc), blk(os_), blk(ow),
                  pl.BlockSpec(w.shape, lambda i: (0, 0))],
        out_specs=pl.BlockSpec((tm, d), lambda i: (i, 0)),
        out_shape=jax.ShapeDtypeStruct((s, d), F32),
        compiler_params=_cparams(("parallel",)),
        name="outproj",
    )(x, oa, ob, oc, os_, ow, w)


def _ffn_kernel(te_ref, nt_ref, x_ref, g_ref, wg_ref, wu_ref, wd_ref, o_ref, h_ref, acc_ref, *, residual):
    t = pl.program_id(0)
    j = pl.program_id(1)

    @pl.when(t < nt_ref[0])
    def _():
        @pl.when(j == 0)
        def _():
            h_ref[...] = _rms(x_ref[...], g_ref[...]).astype(BF16)
            acc_ref[...] = jnp.zeros(acc_ref.shape, F32)

        h = h_ref[...]
        a = jnp.dot(h, wg_ref[0], preferred_element_type=F32)
        u = jnp.dot(h, wu_ref[0], preferred_element_type=F32)
        z = (a * jax.nn.sigmoid(a) * u).astype(BF16)
        acc_ref[...] += jnp.dot(z, wd_ref[0], preferred_element_type=F32)

        @pl.when(j == pl.num_programs(1) - 1)
        def _():
            if residual:
                o_ref[...] = x_ref[...] + acc_ref[...]
            else:
                o_ref[...] = acc_ref[...]

    @pl.when(t >= nt_ref[0])
    def _():
        o_ref[...] = jnp.zeros(o_ref.shape, F32)


def _ffn(x, g, wg, wu, wd, tile_expert, n_tiles, *, residual, tm):
    r, d = x.shape
    f = wg.shape[2]
    tf = 512 if f % 512 == 0 else f
    grid_spec = pltpu.PrefetchScalarGridSpec(
        num_scalar_prefetch=2,
        grid=(r // tm, f // tf),
        in_specs=[pl.BlockSpec((tm, d), lambda t, j, te, nt: (t, 0)),
                  pl.BlockSpec((1, d), lambda t, j, te, nt: (0, 0)),
                  pl.BlockSpec((1, d, tf), lambda t, j, te, nt: (te[t], 0, j)),
                  pl.BlockSpec((1, d, tf), lambda t, j, te, nt: (te[t], 0, j)),
                  pl.BlockSpec((1, tf, d), lambda t, j, te, nt: (te[t], j, 0))],
        out_specs=pl.BlockSpec((tm, d), lambda t, j, te, nt: (t, 0)),
        scratch_shapes=[pltpu.VMEM((tm, d), BF16), pltpu.VMEM((tm, d), F32)])
    return pl.pallas_call(
        functools.partial(_ffn_kernel, residual=residual),
        grid_spec=grid_spec,
        out_shape=jax.ShapeDtypeStruct((r, d), F32),
        compiler_params=_cparams(("parallel", "arbitrary")),
        name="ffn",
    )(tile_expert, n_tiles, x, g, wg, wu, wd)


def _router_kernel(x_ref, g_ref, w_ref, b_ref, o_ref):
    h = _rms(x_ref[...], g_ref[...])
    w = w_ref[...]
    w_hi = w.astype(BF16)
    w_lo = (w - w_hi.astype(F32)).astype(BF16)
    h_hi = h.astype(BF16)
    h_lo = (h - h_hi.astype(F32)).astype(BF16)
    logits = (jnp.dot(h_hi, w_hi, preferred_element_type=F32)
              + jnp.dot(h_lo, w_hi, preferred_element_type=F32)
              + jnp.dot(h_hi, w_lo, preferred_element_type=F32)) + b_ref[...]
    lane = lax.broadcasted_iota(jnp.int32, logits.shape, 1)
    logits = jnp.where(lane < N_EXPERTS, logits, NEG)
    l1 = jnp.max(logits, axis=1, keepdims=True)
    i1 = jnp.min(jnp.where(logits == l1, lane, LANE), axis=1, keepdims=True)
    rest = jnp.where(lane == i1, NEG, logits)
    l2 = jnp.max(rest, axis=1, keepdims=True)
    i2 = jnp.min(jnp.where(rest == l2, lane, LANE), axis=1, keepdims=True)
    e2 = jnp.exp(l2 - l1)
    w1 = 1.0 / (1.0 + e2)
    w2 = e2 / (1.0 + e2)
    out = jnp.where(lane == 0, i1.astype(F32),
                    jnp.where(lane == 1, i2.astype(F32),
                              jnp.where(lane == 2, w1, jnp.where(lane == 3, w2, 0.0))))
    o_ref[...] = out


def _router(x, g, w, b):
    s, d = x.shape
    tm = min(512, s)
    return pl.pallas_call(
        _router_kernel,
        grid=(s // tm,),
        in_specs=[pl.BlockSpec((tm, d), lambda i: (i, 0)),
                  pl.BlockSpec((1, d), lambda i: (0, 0)),
                  pl.BlockSpec((d, LANE), lambda i: (0, 0)),
                  pl.BlockSpec((1, LANE), lambda i: (0, 0))],
        out_specs=pl.BlockSpec((tm, LANE), lambda i: (i, 0)),
        out_shape=jax.ShapeDtypeStruct((s, LANE), F32),
        compiler_params=_cparams(("parallel",)),
        name="router",
    )(x, g, w, b)


def _row_copy(src_hbm, row, dst_ref, r, sem):
    return pltpu.make_async_copy(src_hbm.at[pl.ds(row, 1), :], dst_ref.at[pl.ds(r, 1), :], sem)


ISSUE_UNROLL = 8


def _gather_kernel(idx_ref, x_hbm, o_ref, sem, *, tm):
    base = pl.program_id(0) * tm

    def issue(c, carry):
        for u in range(ISSUE_UNROLL):
            r = c * ISSUE_UNROLL + u
            _row_copy(x_hbm, idx_ref[base + r], o_ref, r, sem).start()
        return carry

    lax.fori_loop(0, tm // ISSUE_UNROLL, issue, 0)

    def wait(r, c):
        _row_copy(x_hbm, 0, o_ref, r, sem).wait()
        return c

    lax.fori_loop(0, tm, wait, 0)


def _gather_rows(x, idx, tm):
    d = x.shape[1]
    n = idx.shape[0]
    grid_spec = pltpu.PrefetchScalarGridSpec(
        num_scalar_prefetch=1,
        grid=(n // tm,),
        in_specs=[pl.BlockSpec(memory_space=pl.ANY)],
        out_specs=pl.BlockSpec((tm, d), lambda i, idx: (i, 0)),
        scratch_shapes=[pltpu.SemaphoreType.DMA(())])
    return pl.pallas_call(
        functools.partial(_gather_kernel, tm=tm),
        grid_spec=grid_spec,
        out_shape=jax.ShapeDtypeStruct((n, d), x.dtype),
        compiler_params=_cparams(("arbitrary",)),
        name="moe_gather",
    )(idx, x)


def _combine_kernel(d0_ref, d1_ref, x_ref, r_ref, y_hbm, o_ref, b0_ref, b1_ref, sem, *, tm):
    base = pl.program_id(0) * tm

    def issue(c, carry):
        for u in range(ISSUE_UNROLL):
            r = c * ISSUE_UNROLL + u
            _row_copy(y_hbm, d0_ref[base + r], b0_ref, r, sem).start()
            _row_copy(y_hbm, d1_ref[base + r], b1_ref, r, sem).start()
        return carry

    lax.fori_loop(0, tm // ISSUE_UNROLL, issue, 0)

    def wait(r, c):
        _row_copy(y_hbm, 0, b0_ref, r, sem).wait()
        _row_copy(y_hbm, 0, b1_ref, r, sem).wait()
        return c

    lax.fori_loop(0, tm, wait, 0)
    rt = r_ref[...]
    o_ref[...] = x_ref[...] + rt[:, 2:3] * b0_ref[...] + rt[:, 3:4] * b1_ref[...]


def _combine(x, route, y, d0, d1):
    s, d = x.shape
    tm = min(256, s)
    grid_spec = pltpu.PrefetchScalarGridSpec(
        num_scalar_prefetch=2,
        grid=(s // tm,),
        in_specs=[pl.BlockSpec((tm, d), lambda i, a, b: (i, 0)),
                  pl.BlockSpec((tm, LANE), lambda i, a, b: (i, 0)),
                  pl.BlockSpec(memory_space=pl.ANY)],
        out_specs=pl.BlockSpec((tm, d), lambda i, a, b: (i, 0)),
        scratch_shapes=[pltpu.VMEM((tm, d), F32), pltpu.VMEM((tm, d), F32),
                        pltpu.SemaphoreType.DMA(())])
    return pl.pallas_call(
        functools.partial(_combine_kernel, tm=tm),
        grid_spec=grid_spec,
        out_shape=jax.ShapeDtypeStruct((s, d), F32),
        compiler_params=_cparams(("arbitrary",)),
        name="moe_combine",
    )(d0, d1, x, route, y)


def _moe(x, g, w_router, b_router, wg, wu, wd):
    s, d = x.shape
    tm = min(512, s)
    wr = jnp.zeros((d, LANE), F32).at[:, :N_EXPERTS].set(w_router)
    br = jnp.zeros((1, LANE), F32).at[0, :N_EXPERTS].set(b_router)
    route = _router(x, g, wr, br)
    e = route[:, :2].astype(jnp.int32).reshape(-1)
    onehot = (e[:, None] == jnp.arange(N_EXPERTS)[None, :]).astype(jnp.int32)
    rank = jnp.sum((jnp.cumsum(onehot, axis=0) - onehot) * onehot, axis=1)
    counts = jnp.sum(onehot, axis=0)
    padded = ((counts + tm - 1) // tm) * tm
    ends = jnp.cumsum(padded)
    dest = (ends - padded)[e] + rank
    n_rows = 2 * s + N_EXPERTS * tm
    row_token = jnp.zeros((n_rows,), jnp.int32).at[dest].set(jnp.arange(2 * s, dtype=jnp.int32) // 2)
    tile_start = jnp.arange(n_rows // tm, dtype=jnp.int32) * tm
    tile_expert = jnp.minimum(jnp.sum(tile_start[:, None] >= ends[None, :], axis=1), N_EXPERTS - 1)
    n_tiles = (ends[-1] // tm).astype(jnp.int32).reshape(1)
    xs = _gather_rows(x, row_token, tm)
    ys = _ffn(xs, g, wg, wu, wd, tile_expert.astype(jnp.int32), n_tiles, residual=False, tm=tm)
    dest2 = dest.reshape(s, 2).astype(jnp.int32)
    return _combine(x, route, ys, dest2[:, 0], dest2[:, 1])


def _norm_kernel(x_ref, g_ref, o_ref):
    o_ref[...] = _rms(x_ref[...], g_ref[...])


def _final_norm(x, g):
    s, d = x.shape
    tm = min(512, s)
    return pl.pallas_call(
        _norm_kernel,
        grid=(s // tm,),
        in_specs=[pl.BlockSpec((tm, d), lambda i: (i, 0)), pl.BlockSpec((1, d), lambda i: (0, 0))],
        out_specs=pl.BlockSpec((tm, d), lambda i: (i, 0)),
        out_shape=jax.ShapeDtypeStruct((s, d), F32),
        compiler_params=_cparams(("parallel",)),
        name="final_norm",
    )(x, g)


def kernel(x, ln_attn, w_in, w_out, lam_q1, lam_k1, lam_q2, lam_k2, subln,
           cmp_pos_k, cmp_w1_k, cmp_w2_k, cmp_pos_v, cmp_w1_v, cmp_w2_v,
           ln_ffn, ffn_w_gate, ffn_w_up, ffn_w_down,
           router_w, router_b, exp_w_gate, exp_w_up, exp_w_down, ln_final):
    bsz, s, d = x.shape
    assert bsz == 1 and s % T_BIG == 0
    depth = w_in.shape[0]
    tb, tw = T_BIG, T_WIN
    n_cmp = s // CMP_STRIDE
    nt = s // tb

    in_cols = _in_col_perm(d)
    col_scale = (in_cols >= 0).astype(np.float32)
    for q0, nq in ((BLK_AQ, A_HEADS), (BLK_BQ, B_HEADS // 2), (BLK_CQ, C_PAIRS)):
        col_scale[q0 * LANE:(q0 + nq) * LANE] *= QK_SCALE * LOG2E
    col_scale = jnp.asarray(col_scale)
    in_cols = jnp.asarray(np.maximum(in_cols, 0))
    out_rows = jnp.asarray(_out_row_perm())

    np_slopes_a = (np.repeat(_alibi(A_HEADS), 2) * LOG2E).astype(np.float32)
    np_slopes_b = (_alibi(B_HEADS) * LOG2E).astype(np.float32)
    sl_c = _alibi(C_HEADS) * LOG2E
    np_slopes_c = np.stack([sl_c[:C_PAIRS], sl_c[C_PAIRS:]], axis=1).reshape(-1).astype(np.float32)
    slopes_a, slopes_b, slopes_c = (jnp.asarray(v) for v in (np_slopes_a, np_slopes_b, np_slopes_c))
    tab_causal = jnp.asarray(_causal_table(tb))
    tab_dil = jnp.asarray(_dilated_table(tb))
    tab_win = jnp.asarray(_window_table(tw))
    ex_sel = jnp.asarray(_select_expand_table(tb), BF16)
    eg = jnp.asarray(_gate_expand_table(), BF16)
    n_slc_pad = -(-(s // SLC_LEN) // LANE) * LANE
    mtab = jnp.asarray(_cmp_to_slc_table(n_cmp, n_slc_pad).T, BF16)
    one_tile = jnp.zeros((s // min(512, s),), jnp.int32)
    tile_cols = np.arange(tb)
    cb_a = jnp.asarray(_alibi_key_lanes(np_slopes_a, tile_cols))
    cb_b = jnp.asarray(_alibi_key_lanes(np_slopes_b, tile_cols))
    ind_ab = jnp.asarray(_alibi_query_lanes(tb), BF16)
    tq_c = min(TQ_CMP, s)
    cmp_end = np.arange(n_cmp) * CMP_STRIDE + (CMP_LEN - 1)
    cb_c = _alibi_key_lanes(np_slopes_c, cmp_end)[:, :, :6]
    cb_c = jnp.asarray(np.pad(cb_c.transpose(1, 0, 2).reshape(n_cmp, 6 * C_PAIRS),
                              ((0, 0), (0, LANE - 6 * C_PAIRS))))
    ind_c = np.zeros((C_PAIRS, 2 * tq_c, LANE), np.float32)
    for b in range(C_PAIRS):
        ind_c[b, :tq_c, 6 * b:6 * b + 3] = 1.0
        ind_c[b, tq_c:, 6 * b + 3:6 * b + 6] = 1.0
    ind_c = jnp.asarray(ind_c, BF16)
    ones_c = jnp.ones((n_cmp, LANE), BF16)

    def cg_spec(t):
        return pl.BlockSpec((None, t, LANE), lambda p, i, fl, sl: (BLK_CG, i, 0))

    eg_spec = pl.BlockSpec((C_PAIRS, LANE, LANE), lambda p, i, fl, sl: (0, 0, 0))
    cb_spec = pl.BlockSpec((1, tb, LANE), lambda p, i, fl, sl: (p, 0, 0))
    ind_spec = pl.BlockSpec((2 * tb, LANE), lambda p, i, fl, sl: (0, 0))

    xs = x[0]
    for l in range(depth):
        w = (jnp.take(w_in[l], in_cols, axis=1) * col_scale[None, :]).astype(BF16)
        proj = _inproj(xs, ln_attn[l][None, :], w)

        lam_init = 0.8 - 0.6 * math.exp(-0.3 * l)
        lam_tab = jnp.zeros((8, LANE), F32).at[:4, :HEAD_DIM].set(
            jnp.stack([lam_q1[l], lam_k1[l], lam_q2[l], lam_k2[l]])).at[4, :].set(lam_init)
        oa = _flash(proj, slopes_a, tab_causal, mode="diff", qblk=BLK_AQ, kblk=BLK_AK, vblk=BLK_AV,
                    npi=1, n_outer=A_HEADS, kv_per_outer=True,
                    extra=(lam_tab, subln[l][None, :], cb_a, ind_ab),
                    extra_specs=(pl.BlockSpec((8, LANE), lambda p, i, fl, sl: (0, 0)),
                                 pl.BlockSpec((1, LANE), lambda p, i, fl, sl: (0, 0)),
                                 cb_spec, ind_spec))

        ob = _flash(proj, slopes_b, tab_dil, mode="pair", qblk=BLK_BQ, kblk=BLK_BK, vblk=BLK_BV,
                    npi=1, n_outer=B_HEADS // 2, kv_per_outer=True, max_ob=tab_dil.shape[0] - 1,
                    extra=(cb_b, ind_ab), extra_specs=(cb_spec, ind_spec))

        pc = proj[BLK_CKC:BLK_CVC + 1].reshape(2, n_cmp, CMP_STRIDE, C_GROUPS, HEAD_DIM)
        cin = pc.transpose(0, 3, 1, 2, 4).reshape(2 * C_GROUPS, n_cmp, CMP_STRIDE * HEAD_DIM)
        pos = jnp.stack([cmp_pos_k[l], cmp_pos_v[l]]).reshape(2, 1, CMP_LEN * HEAD_DIM)
        w1 = jnp.stack([cmp_w1_k[l], cmp_w1_v[l]]).astype(BF16)
        w2 = jnp.zeros((2, LANE, LANE), F32).at[:, :, :HEAD_DIM].set(
            jnp.stack([cmp_w2_k[l], cmp_w2_v[l]])).astype(BF16)
        cmp = _compress(cin, pos, w1, w2)
        kc = jnp.concatenate([cmp[0, :, :HEAD_DIM].astype(BF16), cmp[1, :, :HEAD_DIM].astype(BF16),
                              cb_c], axis=1)
        vc = jnp.concatenate([cmp[2, :, :HEAD_DIM].astype(BF16), cmp[3, :, :HEAD_DIM].astype(BF16),
                              ones_c], axis=1)
        oc, selm, anyblk = _cmp_select(proj, kc, vc, ind_c, eg[0], mtab)
        flags = anyblk[:, 0, :s // SLC_LEN].reshape(nt, -1, nt, tb // SLC_LEN).max(axis=(1, 3)).reshape(-1)
        os_ = _flash(proj, slopes_c, tab_causal, mode="select", qblk=BLK_CQ, kblk=BLK_CKS, vblk=BLK_CVS,
                     npi=C_PAIRS, n_outer=1, kv_per_outer=False, flags=flags,
                     extra=(proj, eg[1], selm, ex_sel),
                     extra_specs=(cg_spec(tb), eg_spec,
                                  pl.BlockSpec((C_GROUPS, selm.shape[1], tb, LANE),
                                               lambda p, i, fl, sl: (0, 0, i, 0)),
                                  pl.BlockSpec(ex_sel.shape, lambda p, i, fl, sl: (0, 0, 0))))
        ow = _flash(proj, slopes_c, tab_win, mode="window", qblk=BLK_CQ, kblk=BLK_CKW, vblk=BLK_CVW,
                    npi=C_PAIRS, n_outer=1, kv_per_outer=False, max_ob=tab_win.shape[0] - 1,
                    extra=(proj, eg[2]), extra_specs=(cg_spec(tw), eg_spec))

        xs = _outproj(xs, oa, ob, oc, os_, ow, jnp.take(w_out[l], out_rows, axis=0).astype(BF16))

        j = l // 2
        g_ffn = ln_ffn[l][None, :]
        if l % 2 == 0:
            n_t = jnp.full((1,), one_tile.shape[0], jnp.int32)
            xs = _ffn(xs, g_ffn, ffn_w_gate[j][None].astype(BF16), ffn_w_up[j][None].astype(BF16),
                      ffn_w_down[j][None].astype(BF16), one_tile, n_t, residual=True, tm=min(512, s))
        else:
            xs = _moe(xs, g_ffn, router_w[j], router_b[j], exp_w_gate[j].astype(BF16),
                      exp_w_up[j].astype(BF16), exp_w_down[j].astype(BF16))
    return _final_norm(xs, ln_final[None, :])[None]
```

```python
import functools
import math

import numpy as np
import jax
import jax.numpy as jnp
from jax import lax
from jax.experimental import pallas as pl
from jax.experimental.pallas import tpu as pltpu

F32 = jnp.float32
BF16 = jnp.bfloat16

HEAD_DIM = 64
LANE = 128
A_HEADS = 4
B_HEADS = 8
C_HEADS = 16
C_GROUPS = 2
C_PAIRS = C_HEADS // C_GROUPS
N_BRANCH = 3
CMP_LEN = 32
CMP_STRIDE = 16
SLC_LEN = 64
N_SELECT = 16
WIN = 512
DILATED_PATTERNS = ((128, 1), (512, 4), (2048, 16))
N_EXPERTS = 8
EPS = 1e-6
NEG = -1e30
FORCE_SCORE = 1e6
QK_SCALE = HEAD_DIM ** -0.5
LOG2E = math.log2(math.e)

BLK_AQ, BLK_AK, BLK_AV = 0, 4, 8
BLK_BQ, BLK_BK, BLK_BV = 12, 16, 20
BLK_CQ = 24
BLK_CKC, BLK_CVC, BLK_CKS, BLK_CVS, BLK_CKW, BLK_CVW, BLK_CG = 32, 33, 34, 35, 36, 37, 38
N_BLK = 40

T_BIG = 512
T_WIN = 256
T_SEL = 256
TQ_CMP = 128
TILE_UNROLL = 8
VMEM_LIMIT = 56 * 1024 * 1024


def _cparams(sem):
    return pltpu.CompilerParams(dimension_semantics=sem, vmem_limit_bytes=VMEM_LIMIT)


def _in_col_perm(d_mix):
    a_w = A_HEADS * HEAD_DIM
    sizes = [a_w, a_w, a_w, a_w, A_HEADS * 2 * HEAD_DIM,
             B_HEADS * HEAD_DIM, B_HEADS * HEAD_DIM, B_HEADS * HEAD_DIM,
             C_HEADS * HEAD_DIM] + [C_GROUPS * HEAD_DIM] * 6 + [C_HEADS * N_BRANCH]
    off = np.concatenate([[0], np.cumsum(sizes)])
    (aq1, aq2, ak1, ak2, av, bq, bk, bv, cq, ckc, cvc, cks, cvs, ckw, cvw, cg) = off[:-1]
    r = np.arange
    cols = []
    for h in range(A_HEADS):
        cols += [aq1 + h * 64 + r(64), aq2 + h * 64 + r(64)]
    for h in range(A_HEADS):
        cols += [ak1 + h * 64 + r(64), ak2 + h * 64 + r(64)]
    for h in range(A_HEADS):
        cols += [av + h * 128 + r(128)]
    for base in (bq, bk, bv):
        cols += [base + r(B_HEADS * HEAD_DIM)]
    for j in range(C_PAIRS):
        cols += [cq + j * 64 + r(64), cq + (C_PAIRS + j) * 64 + r(64)]
    for base in (ckc, cvc, cks, cvs, ckw, cvw):
        cols += [base + r(128)]
    cols += [cg + r(C_HEADS * N_BRANCH), -np.ones(LANE - C_HEADS * N_BRANCH, np.int64)]
    cols += [-np.ones(LANE, np.int64)]
    cols = np.concatenate(cols)
    assert cols.shape[0] == N_BLK * LANE and off[-1] == cols.max() + 1
    return cols


def _out_row_perm():
    rows = [np.arange(A_HEADS * 2 * HEAD_DIM + B_HEADS * HEAD_DIM)]
    c0 = rows[0].shape[0]
    for j in range(C_PAIRS):
        rows += [c0 + j * 64 + np.arange(64), c0 + (C_PAIRS + j) * 64 + np.arange(64)]
    return np.concatenate(rows)


def _alibi(n):
    return np.exp2(-8.0 * np.arange(1, n + 1, dtype=np.float64) / n).astype(np.float32)


def _causal_table(t):
    d = np.arange(t)[:, None] - np.arange(t)[None, :]
    return np.where(d >= 0, 0.0, NEG).astype(np.float32)[None]


def _dilated_table(t):
    n_ob = max(w for w, _ in DILATED_PATTERNS) // t + 1
    tabs = []
    for ob in range(n_ob):
        d = ob * t + np.arange(t)[:, None] - np.arange(t)[None, :]
        c = np.zeros(d.shape, np.float64)
        for w, dil in DILATED_PATTERNS:
            c += (d >= 0) & (d <= w) & (d % dil == 0)
        tabs.append(np.where(c > 0, np.log2(np.maximum(c, 1.0)), NEG))
    return np.stack(tabs).astype(np.float32)


def _window_table(t):
    n_ob = (WIN - 1 + t - 1) // t + 1
    tabs = []
    for ob in range(n_ob):
        d = ob * t + np.arange(t)[:, None] - np.arange(t)[None, :]
        tabs.append(np.where((d >= 0) & (d < WIN), 0.0, NEG))
    return np.stack(tabs).astype(np.float32)


def _select_expand_table(t):
    per = t // SLC_LEN
    n = LANE // per
    e = np.zeros((n, t, LANE), np.float32)
    for jj in range(n):
        e[jj, np.arange(t), jj * per + np.arange(t) // SLC_LEN] = 1e30
    return e


def _split3(v):
    v = np.asarray(v, np.float32)
    hi = v.astype(BF16)
    r = v - hi.astype(np.float32)
    mid = r.astype(BF16)
    lo = (r - mid.astype(np.float32)).astype(BF16)
    return hi, mid, lo


def _alibi_key_lanes(slopes2, cols):
    n_pairs = slopes2.shape[0] // 2
    v = (slopes2.astype(np.float32).reshape(n_pairs, 2, 1)
         * cols.astype(np.float32).reshape(1, 1, -1))
    parts = np.stack(_split3(v), axis=-1)
    lanes = parts.transpose(0, 2, 1, 3).reshape(n_pairs, -1, 6)
    return np.pad(lanes, ((0, 0), (0, 0), (0, LANE - 6)))


def _alibi_query_lanes(t):
    e = np.zeros((2 * t, LANE), np.float32)
    e[:t, 0:3] = 1.0
    e[t:, 3:6] = 1.0
    return e


def _gate_expand_table():
    e = np.zeros((N_BRANCH, C_PAIRS, LANE, LANE), np.float32)
    for br in range(N_BRANCH):
        for j in range(C_PAIRS):
            e[br, j, j * N_BRANCH + br, :64] = 1.0
            e[br, j, (C_PAIRS + j) * N_BRANCH + br, 64:] = 1.0
    return e


def _cmp_to_slc_table(n_cmp, n_slc):
    r = SLC_LEN // CMP_STRIDE
    m = np.zeros((n_cmp, n_slc), np.float32)
    for b in range(n_slc):
        for n in range(r * b - 1, r * b + r):
            if 0 <= n < n_cmp:
                m[n, b] = 1.0
    return m


def _rms(x, g):
    return x * lax.rsqrt(jnp.mean(x * x, axis=-1, keepdims=True) + EPS) * g


def _split_halves(q):
    lane = lax.broadcasted_iota(jnp.int32, q.shape, 1)
    zero = jnp.zeros_like(q)
    return jnp.concatenate([jnp.where(lane < HEAD_DIM, q, zero),
                            jnp.where(lane >= HEAD_DIM, q, zero)], axis=0)


def _merge_halves(o2, t):
    lane = lax.broadcasted_iota(jnp.int32, (t, LANE), 1)
    return jnp.where(lane < HEAD_DIM, o2[:t], o2[t:])


def _dot_nt(a, b):
    return lax.dot_general(a, b, (((1,), (1,)), ((), ())), preferred_element_type=F32)


def _dot_hilo(x, w):
    hi = x.astype(BF16)
    lo = (x - hi.astype(F32)).astype(BF16)
    return (jnp.dot(hi, w, preferred_element_type=F32)
            + jnp.dot(lo, w, preferred_element_type=F32))


def _gate_lanes(cg, eg):
    return _dot_hilo(jax.nn.sigmoid(cg.astype(F32)), eg)


def _inproj_kernel(x_ref, g_ref, w_ref, o_ref, h_ref, *, nb):
    @pl.when(pl.program_id(1) == 0)
    def _():
        h_ref[...] = _rms(x_ref[...], g_ref[...]).astype(BF16)

    h = h_ref[...]
    for c in range(nb // 2):
        r = jnp.dot(h, w_ref[:, c * 256:(c + 1) * 256], preferred_element_type=F32)
        o_ref[2 * c] = r[:, :LANE].astype(BF16)
        o_ref[2 * c + 1] = r[:, LANE:].astype(BF16)


def _inproj(x, g, w):
    s, d = x.shape
    tm, nb = 1024, 10
    tm = min(tm, s)
    return pl.pallas_call(
        functools.partial(_inproj_kernel, nb=nb),
        grid=(s // tm, N_BLK // nb),
        in_specs=[pl.BlockSpec((tm, d), lambda i, j: (i, 0)),
                  pl.BlockSpec((1, d), lambda i, j: (0, 0)),
                  pl.BlockSpec((d, nb * LANE), lambda i, j: (0, j))],
        out_specs=pl.BlockSpec((nb, tm, LANE), lambda i, j: (j, i, 0)),
        out_shape=jax.ShapeDtypeStruct((N_BLK, s, LANE), BF16),
        scratch_shapes=[pltpu.VMEM((tm, d), BF16)],
        compiler_params=_cparams(("parallel", "arbitrary")),
        name="inproj",
    )(x, g, w)


def _flash_kernel(*refs, mode, npi, t, nt, n_tab, max_ob):
    it = iter(refs)
    flag_ref, slope_ref = next(it), next(it)
    q_ref, k_ref, v_ref, tab_ref = next(it), next(it), next(it), next(it)
    if mode == "diff":
        lam_ref, gsub_ref = next(it), next(it)
    if mode in ("diff", "pair"):
        cb_ref, ind_ref = next(it), next(it)
    if mode in ("select", "window"):
        cg_ref, eg_ref = next(it), next(it)
    if mode == "select":
        sel_ref, ext_ref = next(it), next(it)
    o_ref = next(it)
    q2_ref, m_ref, acc_ref = next(it), next(it), next(it)

    alibi_in_matmul = mode in ("diff", "pair")
    p = pl.program_id(0)
    i = pl.program_id(1)
    rg = 2 * t
    reps = t // LANE
    for b in range(npi):
        q2 = _split_halves(q_ref[b])
        if alibi_in_matmul:
            q2 = jnp.concatenate([q2, ind_ref[...]], axis=1)
        q2_ref[b * rg:(b + 1) * rg] = q2
    m_ref[...] = jnp.full(m_ref.shape, NEG, F32)
    acc_ref[...] = jnp.zeros(acc_ref.shape, F32)
    ones = jnp.ones((t, LANE), BF16)

    def step(j, ob):
        ks = pl.multiple_of(j * t, t)
        k = k_ref[pl.ds(ks, t), :]
        vaug = jnp.concatenate([v_ref[pl.ds(ks, t), :], ones], axis=1)
        tile_off = ((j - i) * t).astype(F32)
        colv = lax.broadcasted_iota(jnp.int32, (1, t), 1).astype(F32) + tile_off
        if mode == "select":
            per = LANE * SLC_LEN // t
            k = jnp.concatenate([k, ext_ref[j % per]], axis=1)
            selcat = jnp.concatenate([sel_ref[0, j // per], sel_ref[1, j // per]], axis=0)
        for b in range(npi):
            qa = q2_ref[b * rg:(b + 1) * rg]
            if mode == "select":
                qa = jnp.concatenate([qa, selcat], axis=1)
            ka = jnp.concatenate([k, cb_ref[b]], axis=1) if alibi_in_matmul else k
            s = _dot_nt(qa, ka)
            ps, alphas = [], []
            for hf in range(2):
                g0 = b * rg + hf * t
                slope = slope_ref[(p * npi + b) * 2 + hf]
                sb = s[hf * t:(hf + 1) * t]
                if not alibi_in_matmul:
                    sb = sb + slope * colv
                if ob is not None:
                    sb = sb + tab_ref[ob]
                m_prev = m_ref[g0:g0 + t]
                row_max = jnp.max(sb, axis=1, keepdims=True)
                if alibi_in_matmul:
                    off = slope * tile_off
                    m_new = jnp.maximum(m_prev, row_max + off)
                    m_loc = m_new - off
                else:
                    m_new = jnp.maximum(m_prev, row_max)
                    m_loc = m_new
                alphas.append(jnp.exp2(m_prev - m_new))
                m_ref[g0:g0 + t] = m_new
                mb = jnp.concatenate([m_loc] * reps, axis=1)
                ps.append(jnp.exp2(sb - mb).astype(BF16))
            al = jnp.concatenate(alphas, axis=0)
            pv = jnp.dot(jnp.concatenate(ps, axis=0), vaug, preferred_element_type=F32)
            acc_ref[b * rg:(b + 1) * rg] = (jnp.concatenate([al, al], axis=1)
                                            * acc_ref[b * rg:(b + 1) * rg] + pv)

    j_lo = 0 if max_ob is None else jnp.maximum(i - max_ob, 0)
    j_mid = jnp.maximum(i - n_tab + 1, j_lo)
    if mode == "select":
        def body(j, c):
            @pl.when(flag_ref[i * nt + j] > 0)
            def _():
                step(j, None)
            return c

        lax.fori_loop(j_lo, j_mid, body, 0)
    elif max_ob is None or max_ob >= n_tab:
        n_full = jnp.right_shift(j_mid - j_lo, TILE_UNROLL.bit_length() - 1)

        def body_many(c, carry):
            for u in range(TILE_UNROLL):
                step(j_lo + TILE_UNROLL * c + u, None)
            return carry

        lax.fori_loop(0, n_full, body_many, 0)
        done = j_lo + TILE_UNROLL * n_full
        width = TILE_UNROLL // 2
        while width >= 1:
            @pl.when(jnp.bitwise_and(j_mid - j_lo, width) != 0)
            def _(done=done, width=width):
                for u in range(width):
                    step(done + u, None)

            done = done + jnp.bitwise_and(j_mid - j_lo, width)
            width //= 2
    if npi == 1 and n_tab > 1:
        @pl.when(i >= n_tab - 1)
        def _():
            for ob in reversed(range(n_tab)):
                step(i - ob, ob)
        early = n_tab - 1
    else:
        early = n_tab
    for ob in reversed(range(early)):
        @pl.when((i - ob >= 0) if early == n_tab else ((i - ob >= 0) & (i < n_tab - 1)))
        def _(ob=ob):
            step(i - ob, ob)

    for b in range(npi):
        a = acc_ref[b * rg:(b + 1) * rg]
        o2 = a[:, :LANE] / a[:, LANE:]
        if mode == "diff":
            lam_init = lam_ref[4:5, 0:1]
            lam = (jnp.exp(jnp.sum(lam_ref[0:1] * lam_ref[1:2], keepdims=True))
                   - jnp.exp(jnp.sum(lam_ref[2:3] * lam_ref[3:4], keepdims=True)) + lam_init)
            o = o2[:t] - lam * o2[t:]
            o_ref[b] = (_rms(o, gsub_ref[...]) * (1.0 - lam_init)).astype(BF16)
        else:
            o = _merge_halves(o2, t)
            if mode in ("select", "window"):
                o = o * _gate_lanes(cg_ref[...], eg_ref[b])
            o_ref[b] = o.astype(BF16)


def _flash(proj, slopes, tab, *, mode, qblk, kblk, vblk, npi, n_outer, kv_per_outer,
           max_ob=None, flags=None, extra=(), extra_specs=()):
    _, s, _ = proj.shape
    n_tab, t, _ = tab.shape
    rows = 2 * npi * t
    q_lanes = 2 * LANE if mode in ("diff", "pair") else LANE
    if flags is None:
        flags = jnp.zeros((1,), jnp.int32)
    kvmap = ((lambda b: (lambda p, i, fl, sl: (b + p, 0, 0))) if kv_per_outer
             else (lambda b: (lambda p, i, fl, sl: (b, 0, 0))))
    in_specs = [pl.BlockSpec((npi, t, LANE), lambda p, i, fl, sl: (qblk // npi + p, i, 0)),
                pl.BlockSpec((None, s, LANE), kvmap(kblk)),
                pl.BlockSpec((None, s, LANE), kvmap(vblk)),
                pl.BlockSpec(tab.shape, lambda p, i, fl, sl: (0, 0, 0))] + list(extra_specs)
    grid_spec = pltpu.PrefetchScalarGridSpec(
        num_scalar_prefetch=2,
        grid=(n_outer, s // t),
        in_specs=in_specs,
        out_specs=pl.BlockSpec((npi, t, LANE), lambda p, i, fl, sl: (p, i, 0)),
        scratch_shapes=[pltpu.VMEM((rows, q_lanes), BF16), pltpu.VMEM((rows, LANE), F32),
                        pltpu.VMEM((rows, 2 * LANE), F32)])
    return pl.pallas_call(
        functools.partial(_flash_kernel, mode=mode, npi=npi, t=t, nt=s // t, n_tab=n_tab,
                          max_ob=max_ob),
        grid_spec=grid_spec,
        out_shape=jax.ShapeDtypeStruct((n_outer * npi, s, LANE), BF16),
        compiler_params=_cparams(("parallel", "arbitrary")),
        name="flash_" + mode,
    )(flags, slopes, proj, proj, proj, tab, *extra)


def _compress_kernel(c_ref, pos_ref, w1_ref, w2_ref, o_ref):
    c = c_ref[0].astype(F32)
    half = c.shape[1]
    pos = pos_ref[0]
    xa = (c + pos[:, :half]).astype(BF16)
    xb = (c + pos[:, half:]).astype(BF16)
    u = jnp.dot(xa, w1_ref[0, :half], preferred_element_type=F32)
    v = jnp.dot(xb, w1_ref[0, half:], preferred_element_type=F32)
    hid = u + pltpu.roll(v, shift=v.shape[0] - 1, axis=0)
    act = hid * jax.nn.sigmoid(hid)
    o_ref[0] = jnp.dot(act.astype(BF16), w2_ref[0], preferred_element_type=F32)


def _compress(cin, pos, w1, w2):
    n, nc, width = cin.shape
    return pl.pallas_call(
        _compress_kernel,
        grid=(n,),
        in_specs=[pl.BlockSpec((1, nc, width), lambda a: (a, 0, 0)),
                  pl.BlockSpec((1, 1, 2 * width), lambda a: (a // 2, 0, 0)),
                  pl.BlockSpec((1, 2 * width, LANE), lambda a: (a // 2, 0, 0)),
                  pl.BlockSpec((1, LANE, LANE), lambda a: (a // 2, 0, 0))],
        out_specs=pl.BlockSpec((1, nc, LANE), lambda a: (a, 0, 0)),
        out_shape=jax.ShapeDtypeStruct((n, nc, LANE), F32),
        compiler_params=_cparams(("parallel",)),
        name="nsa_compress",
    )(cin, pos, w1, w2)


def _cmp_select_kernel(q_ref, kc_ref, vc_ref, ind_ref, cg_ref, eg_ref, m_ref,
                       o_ref, sel_ref, any_ref, slc_ref, *, tq, n_cmp, n_slc, n_sel, n_var):
    i = pl.program_id(0)
    t = i * tq + lax.broadcasted_iota(jnp.int32, (tq, 1), 0)
    row_ok = jnp.where(t >= CMP_LEN - 1, 1.0, 0.0)

    def attend(nc):
        cmp_end = lax.broadcasted_iota(jnp.int32, (1, nc), 1) * CMP_STRIDE + (CMP_LEN - 1)
        maskb = jnp.where(t >= cmp_end, 0.0, NEG)
        kc = kc_ref[:nc]
        vc = vc_ref[:nc]
        reps = nc // LANE
        imp = [jnp.zeros((tq, nc), F32), jnp.zeros((tq, nc), F32)]
        for b in range(C_PAIRS):
            qa = jnp.concatenate([_split_halves(q_ref[b]), ind_ref[b]], axis=1)
            s2 = _dot_nt(qa, kc)
            es = []
            for hf in range(2):
                s = s2[hf * tq:(hf + 1) * tq] + maskb
                es.append(jnp.exp2(s - jnp.max(s, axis=1, keepdims=True)))
            o2 = jnp.dot(jnp.concatenate(es, axis=0).astype(BF16), vc, preferred_element_type=F32)
            ok2 = jnp.concatenate([row_ok, row_ok], axis=0)
            rden = ok2 / o2[:, LANE:]
            for hf in range(2):
                r = rden[hf * tq:(hf + 1) * tq]
                imp[hf] = imp[hf] + es[hf] * jnp.concatenate([r] * reps, axis=1)
            o = _merge_halves(o2[:, :LANE] * rden, tq) * _gate_lanes(cg_ref[...], eg_ref[b])
            o_ref[b] = o.astype(BF16)
        imp2 = jnp.concatenate(imp, axis=0)
        imp_hi = imp2.astype(BF16)
        imp_lo = (imp2 - imp_hi.astype(F32)).astype(BF16)
        slc_ref[...] = _dot_nt(m_ref[:, :nc], imp_hi) + _dot_nt(m_ref[:, :nc], imp_lo)

    n_q = pl.num_programs(0)
    for v in range(1, n_var + 1):
        @pl.when((i * n_var >= (v - 1) * n_q) & (i * n_var < v * n_q))
        def _(v=v):
            attend(n_cmp * v // n_var)

    slc = slc_ref[...]
    t_row = i * tq + lax.broadcasted_iota(jnp.int32, (1, tq), 1)
    t2 = jnp.concatenate([t_row, t_row], axis=1)
    blk = lax.broadcasted_iota(jnp.int32, (n_slc, 2 * tq), 0)
    tb = jnp.right_shift(t2, int(math.log2(SLC_LEN)))
    forced = (blk == 0) | (blk == tb) | (blk == tb - 1)
    allowed = blk * SLC_LEN <= t2
    score = jnp.where(forced, -2.0, jnp.where(allowed, slc, -1.0))
    picked = jnp.where(forced, 1, 0)
    for _ in range(max(n_sel - 3, 0)):
        mx = jnp.max(score, axis=0, keepdims=True)
        first = jnp.min(jnp.where(score == mx, blk, n_slc), axis=0, keepdims=True)
        hit = blk == first
        picked = jnp.where(hit, 1, picked)
        score = jnp.where(hit, -2.0, score)
    sel_t = jnp.where(picked > 0, 0.0, -1.0)
    any_t = jnp.where(allowed & (picked > 0), 1.0, 0.0)
    both_t = jnp.concatenate([sel_t, any_t], axis=0).astype(BF16)
    rr = lax.broadcasted_iota(jnp.int32, (2 * tq, 2 * tq), 0)
    cc = lax.broadcasted_iota(jnp.int32, (2 * tq, 2 * tq), 1)
    eye = jnp.where(rr == cc, 1.0, 0.0).astype(BF16)
    both = _dot_nt(eye, both_t)
    selm = both[:, :n_slc].astype(BF16)
    for g in range(C_GROUPS):
        for hf in range(n_slc // LANE):
            sel_ref[g, hf] = selm[g * tq:(g + 1) * tq, hf * LANE:(hf + 1) * LANE]
    any_ref[0] = jnp.max(both[:, n_slc:], axis=0, keepdims=True).astype(jnp.int32)


def _cmp_select(proj, kc, vc, ind, eg, mtab):
    _, s, _ = proj.shape
    tq = ind.shape[1] // 2
    n_cmp = kc.shape[0]
    n_slc = mtab.shape[0]
    n_var = 4 if (n_cmp % (4 * LANE) == 0 and (s // tq) % 4 == 0) else 1
    return pl.pallas_call(
        functools.partial(_cmp_select_kernel, tq=tq, n_cmp=n_cmp, n_slc=n_slc,
                          n_sel=min(N_SELECT, s // SLC_LEN), n_var=n_var),
        grid=(s // tq,),
        in_specs=[pl.BlockSpec((C_PAIRS, tq, LANE), lambda i: (BLK_CQ // C_PAIRS, i, 0)),
                  pl.BlockSpec((n_cmp, 2 * LANE), lambda i: (0, 0)),
                  pl.BlockSpec((n_cmp, 2 * LANE), lambda i: (0, 0)),
                  pl.BlockSpec(ind.shape, lambda i: (0, 0, 0)),
                  pl.BlockSpec((None, tq, LANE), lambda i: (BLK_CG, i, 0)),
                  pl.BlockSpec((C_PAIRS, LANE, LANE), lambda i: (0, 0, 0)),
                  pl.BlockSpec((n_slc, n_cmp), lambda i: (0, 0))],
        out_specs=[pl.BlockSpec((C_PAIRS, tq, LANE), lambda i: (0, i, 0)),
                   pl.BlockSpec((C_GROUPS, n_slc // LANE, tq, LANE), lambda i: (0, 0, i, 0)),
                   pl.BlockSpec((1, 1, n_slc), lambda i: (i, 0, 0))],
        out_shape=[jax.ShapeDtypeStruct((C_PAIRS, s, LANE), BF16),
                   jax.ShapeDtypeStruct((C_GROUPS, n_slc // LANE, s, LANE), BF16),
                   jax.ShapeDtypeStruct((s // tq, 1, n_slc), jnp.int32)],
        scratch_shapes=[pltpu.VMEM((n_slc, 2 * tq), F32)],
        compiler_params=_cparams(("parallel",)),
        name="nsa_cmp_select",
    )(proj, kc, vc, ind, proj, eg, mtab)


def _outproj_kernel(x_ref, oa_ref, ob_ref, oc_ref, os_ref, ow_ref, w_ref, o_ref):
    parts = [oa_ref[b] for b in range(oa_ref.shape[0])]
    parts += [ob_ref[b] for b in range(ob_ref.shape[0])]
    for b in range(oc_ref.shape[0]):
        parts.append((oc_ref[b].astype(F32) + os_ref[b].astype(F32)
                      + ow_ref[b].astype(F32)).astype(BF16))
    mix = jnp.concatenate(parts, axis=1)
    n = o_ref.shape[1]
    for c in range(n // 512):
        sl = slice(c * 512, (c + 1) * 512)
        o_ref[:, sl] = x_ref[:, sl] + jnp.dot(mix, w_ref[:, sl], preferred_element_type=F32)


def _outproj(x, oa, ob, oc, os_, ow, w):
    s, d = x.shape
    tm = min(512, s)
    blk = lambda a: pl.BlockSpec((a.shape[0], tm, LANE), lambda i: (0, i, 0))
    return pl.pallas_call(
        _outproj_kernel,
        grid=(s // tm,),
        in_specs=[pl.BlockSpec((tm, d), lambda i: (i, 0)), blk(oa), blk(ob), blk(oc), blk(os_), blk(ow),
                  pl.BlockSpec(w.shape, lambda i: (0, 0))],
        out_specs=pl.BlockSpec((tm, d), lambda i: (i, 0)),
        out_shape=jax.ShapeDtypeStruct((s, d), F32),
        compiler_params=_cparams(("parallel",)),
        name="outproj",
    )(x, oa, ob, oc, os_, ow, w)


def _ffn_kernel(te_ref, nt_ref, x_ref, g_ref, wg_ref, wu_ref, wd_ref, o_ref, h_ref, acc_ref, *, residual):
    t = pl.program_id(0)
    j = pl.program_id(1)

    @pl.when(t < nt_ref[0])
    def _():
        @pl.when(j == 0)
        def _():
            h_ref[...] = _rms(x_ref[...], g_ref[...]).astype(BF16)
            acc_ref[...] = jnp.zeros(acc_ref.shape, F32)

        h = h_ref[...]
        a = jnp.dot(h, wg_ref[0], preferred_element_type=F32)
        u = jnp.dot(h, wu_ref[0], preferred_element_type=F32)
        z = (a * jax.nn.sigmoid(a) * u).astype(BF16)
        acc_ref[...] += jnp.dot(z, wd_ref[0], preferred_element_type=F32)

        @pl.when(j == pl.num_programs(1) - 1)
        def _():
            if residual:
                o_ref[...] = x_ref[...] + acc_ref[...]
            else:
                o_ref[...] = acc_ref[...]

    @pl.when(t >= nt_ref[0])
    def _():
        o_ref[...] = jnp.zeros(o_ref.shape, F32)


def _ffn(x, g, wg, wu, wd, tile_expert, n_tiles, *, residual, tm):
    r, d = x.shape
    f = wg.shape[2]
    tf = 512 if f % 512 == 0 else f
    grid_spec = pltpu.PrefetchScalarGridSpec(
        num_scalar_prefetch=2,
        grid=(r // tm, f // tf),
        in_specs=[pl.BlockSpec((tm, d), lambda t, j, te, nt: (t, 0)),
                  pl.BlockSpec((1, d), lambda t, j, te, nt: (0, 0)),
                  pl.BlockSpec((1, d, tf), lambda t, j, te, nt: (te[t], 0, j)),
                  pl.BlockSpec((1, d, tf), lambda t, j, te, nt: (te[t], 0, j)),
                  pl.BlockSpec((1, tf, d), lambda t, j, te, nt: (te[t], j, 0))],
        out_specs=pl.BlockSpec((tm, d), lambda t, j, te, nt: (t, 0)),
        scratch_shapes=[pltpu.VMEM((tm, d), BF16), pltpu.VMEM((tm, d), F32)])
    return pl.pallas_call(
        functools.partial(_ffn_kernel, residual=residual),
        grid_spec=grid_spec,
        out_shape=jax.ShapeDtypeStruct((r, d), F32),
        compiler_params=_cparams(("parallel", "arbitrary")),
        name="ffn",
    )(tile_expert, n_tiles, x, g, wg, wu, wd)


def _router_kernel(x_ref, g_ref, w_ref, b_ref, o_ref):
    h = _rms(x_ref[...], g_ref[...])
    w = w_ref[...]
    w_hi = w.astype(BF16)
    w_lo = (w - w_hi.astype(F32)).astype(BF16)
    h_hi = h.astype(BF16)
    h_lo = (h - h_hi.astype(F32)).astype(BF16)
    logits = (jnp.dot(h_hi, w_hi, preferred_element_type=F32)
              + jnp.dot(h_lo, w_hi, preferred_element_type=F32)
              + jnp.dot(h_hi, w_lo, preferred_element_type=F32)) + b_ref[...]
    lane = lax.broadcasted_iota(jnp.int32, logits.shape, 1)
    logits = jnp.where(lane < N_EXPERTS, logits, NEG)
    l1 = jnp.max(logits, axis=1, keepdims=True)
    i1 = jnp.min(jnp.where(logits == l1, lane, LANE), axis=1, keepdims=True)
    rest = jnp.where(lane == i1, NEG, logits)
    l2 = jnp.max(rest, axis=1, keepdims=True)
    i2 = jnp.min(jnp.where(rest == l2, lane, LANE), axis=1, keepdims=True)
    e2 = jnp.exp(l2 - l1)
    w1 = 1.0 / (1.0 + e2)
    w2 = e2 / (1.0 + e2)
    out = jnp.where(lane == 0, i1.astype(F32),
                    jnp.where(lane == 1, i2.astype(F32),
                              jnp.where(lane == 2, w1, jnp.where(lane == 3, w2, 0.0))))
    o_ref[...] = out


def _router(x, g, w, b):
    s, d = x.shape
    tm = min(512, s)
    return pl.pallas_call(
        _router_kernel,
        grid=(s // tm,),
        in_specs=[pl.BlockSpec((tm, d), lambda i: (i, 0)),
                  pl.BlockSpec((1, d), lambda i: (0, 0)),
                  pl.BlockSpec((d, LANE), lambda i: (0, 0)),
                  pl.BlockSpec((1, LANE), lambda i: (0, 0))],
        out_specs=pl.BlockSpec((tm, LANE), lambda i: (i, 0)),
        out_shape=jax.ShapeDtypeStruct((s, LANE), F32),
        compiler_params=_cparams(("parallel",)),
        name="router",
    )(x, g, w, b)


def _row_copy(src_hbm, row, dst_ref, r, sem):
    return pltpu.make_async_copy(src_hbm.at[pl.ds(row, 1), :], dst_ref.at[pl.ds(r, 1), :], sem)


ISSUE_UNROLL = 8


def _gather_kernel(idx_ref, x_hbm, o_ref, sem, *, tm):
    base = pl.program_id(0) * tm

    def issue(c, carry):
        for u in range(ISSUE_UNROLL):
            r = c * ISSUE_UNROLL + u
            _row_copy(x_hbm, idx_ref[base + r], o_ref, r, sem).start()
        return carry

    lax.fori_loop(0, tm // ISSUE_UNROLL, issue, 0)

    def wait(r, c):
        _row_copy(x_hbm, 0, o_ref, r, sem).wait()
        return c

    lax.fori_loop(0, tm, wait, 0)


def _gather_rows(x, idx, tm):
    d = x.shape[1]
    n = idx.shape[0]
    grid_spec = pltpu.PrefetchScalarGridSpec(
        num_scalar_prefetch=1,
        grid=(n // tm,),
        in_specs=[pl.BlockSpec(memory_space=pl.ANY)],
        out_specs=pl.BlockSpec((tm, d), lambda i, idx: (i, 0)),
        scratch_shapes=[pltpu.SemaphoreType.DMA(())])
    return pl.pallas_call(
        functools.partial(_gather_kernel, tm=tm),
        grid_spec=grid_spec,
        out_shape=jax.ShapeDtypeStruct((n, d), x.dtype),
        compiler_params=_cparams(("arbitrary",)),
        name="moe_gather",
    )(idx, x)


def _combine_kernel(d0_ref, d1_ref, x_ref, r_ref, y_hbm, o_ref, b0_ref, b1_ref, sem, *, tm):
    base = pl.program_id(0) * tm

    def issue(c, carry):
        for u in range(ISSUE_UNROLL):
            r = c * ISSUE_UNROLL + u
            _row_copy(y_hbm, d0_ref[base + r], b0_ref, r, sem).start()
            _row_copy(y_hbm, d1_ref[base + r], b1_ref, r, sem).start()
        return carry

    lax.fori_loop(0, tm // ISSUE_UNROLL, issue, 0)

    def wait(r, c):
        _row_copy(y_hbm, 0, b0_ref, r, sem).wait()
        _row_copy(y_hbm, 0, b1_ref, r, sem).wait()
        return c

    lax.fori_loop(0, tm, wait, 0)
    rt = r_ref[...]
    o_ref[...] = x_ref[...] + rt[:, 2:3] * b0_ref[...] + rt[:, 3:4] * b1_ref[...]


def _combine(x, route, y, d0, d1):
    s, d = x.shape
    tm = min(256, s)
    grid_spec = pltpu.PrefetchScalarGridSpec(
        num_scalar_prefetch=2,
        grid=(s // tm,),
        in_specs=[pl.BlockSpec((tm, d), lambda i, a, b: (i, 0)),
                  pl.BlockSpec((tm, LANE), lambda i, a, b: (i, 0)),
                  pl.BlockSpec(memory_space=pl.ANY)],
        out_specs=pl.BlockSpec((tm, d), lambda i, a, b: (i, 0)),
        scratch_shapes=[pltpu.VMEM((tm, d), F32), pltpu.VMEM((tm, d), F32),
                        pltpu.SemaphoreType.DMA(())])
    return pl.pallas_call(
        functools.partial(_combine_kernel, tm=tm),
        grid_spec=grid_spec,
        out_shape=jax.ShapeDtypeStruct((s, d), F32),
        compiler_params=_cparams(("arbitrary",)),
        name="moe_combine",
    )(d0, d1, x, route, y)


def _moe(x, g, w_router, b_router, wg, wu, wd, *, first_expert):
    s, d = x.shape
    tm = min(512, s)
    wr = jnp.zeros((d, LANE), F32).at[:, :N_EXPERTS].set(w_router)
    br = jnp.zeros((1, LANE), F32).at[0, :N_EXPERTS].set(b_router)
    route = _router(x, g, wr, br)
    e = route[:, :2].astype(jnp.int32).reshape(-1)
    onehot = (e[:, None] == jnp.arange(N_EXPERTS)[None, :]).astype(jnp.int32)
    rank = jnp.sum((jnp.cumsum(onehot, axis=0) - onehot) * onehot, axis=1)
    counts = jnp.sum(onehot, axis=0)
    padded = ((counts + tm - 1) // tm) * tm
    ends = jnp.cumsum(padded)
    dest = (ends - padded)[e] + rank
    n_rows = 2 * s + N_EXPERTS * tm
    row_token = jnp.zeros((n_rows,), jnp.int32).at[dest].set(jnp.arange(2 * s, dtype=jnp.int32) // 2)
    tile_start = jnp.arange(n_rows // tm, dtype=jnp.int32) * tm
    tile_expert = jnp.minimum(jnp.sum(tile_start[:, None] >= ends[None, :], axis=1), N_EXPERTS - 1)
    n_tiles = (ends[-1] // tm).astype(jnp.int32).reshape(1)
    xs = _gather_rows(x, row_token, tm)
    ys = _ffn(xs, g, wg, wu, wd, tile_expert.astype(jnp.int32) + first_expert, n_tiles,
              residual=False, tm=tm)
    dest2 = dest.reshape(s, 2).astype(jnp.int32)
    return _combine(x, route, ys, dest2[:, 0], dest2[:, 1])


def _norm_kernel(x_ref, g_ref, o_ref):
    o_ref[...] = _rms(x_ref[...], g_ref[...])


def _final_norm(x, g):
    s, d = x.shape
    tm = min(512, s)
    return pl.pallas_call(
        _norm_kernel,
        grid=(s // tm,),
        in_specs=[pl.BlockSpec((tm, d), lambda i: (i, 0)), pl.BlockSpec((1, d), lambda i: (0, 0))],
        out_specs=pl.BlockSpec((tm, d), lambda i: (i, 0)),
        out_shape=jax.ShapeDtypeStruct((s, d), F32),
        compiler_params=_cparams(("parallel",)),
        name="final_norm",
    )(x, g)


def kernel(x, ln_attn, w_in, w_out, lam_q1, lam_k1, lam_q2, lam_k2, subln,
           cmp_pos_k, cmp_w1_k, cmp_w2_k, cmp_pos_v, cmp_w1_v, cmp_w2_v,
           ln_ffn, ffn_w_gate, ffn_w_up, ffn_w_down,
           router_w, router_b, exp_w_gate, exp_w_up, exp_w_down, ln_final):
    bsz, s, d = x.shape
    assert bsz == 1 and s % T_BIG == 0
    depth = w_in.shape[0]
    tb, tw = T_BIG, T_WIN
    n_cmp = s // CMP_STRIDE
    nt = s // tb

    in_cols = _in_col_perm(d)
    col_scale = (in_cols >= 0).astype(np.float32)
    for q0, nq in ((BLK_AQ, A_HEADS), (BLK_BQ, B_HEADS // 2), (BLK_CQ, C_PAIRS)):
        col_scale[q0 * LANE:(q0 + nq) * LANE] *= QK_SCALE * LOG2E
    col_scale = jnp.asarray(col_scale)
    in_cols = jnp.asarray(np.maximum(in_cols, 0))
    out_rows = jnp.asarray(_out_row_perm())

    np_slopes_a = (np.repeat(_alibi(A_HEADS), 2) * LOG2E).astype(np.float32)
    np_slopes_b = (_alibi(B_HEADS) * LOG2E).astype(np.float32)
    sl_c = _alibi(C_HEADS) * LOG2E
    np_slopes_c = np.stack([sl_c[:C_PAIRS], sl_c[C_PAIRS:]], axis=1).reshape(-1).astype(np.float32)
    slopes_a, slopes_b, slopes_c = (jnp.asarray(v) for v in (np_slopes_a, np_slopes_b, np_slopes_c))
    tab_causal = jnp.asarray(_causal_table(tb))
    tab_dil = jnp.asarray(_dilated_table(tb))
    tab_win = jnp.asarray(_window_table(tw))
    ts = T_SEL
    tab_causal_sel = jnp.asarray(_causal_table(ts))
    ex_sel = jnp.asarray(_select_expand_table(ts), BF16)
    eg = jnp.asarray(_gate_expand_table(), BF16)
    n_slc_pad = -(-(s // SLC_LEN) // LANE) * LANE
    mtab = jnp.asarray(_cmp_to_slc_table(n_cmp, n_slc_pad).T, BF16)
    one_tile = jnp.zeros((s // min(512, s),), jnp.int32)
    tile_cols = np.arange(tb)
    cb_a = jnp.asarray(_alibi_key_lanes(np_slopes_a, tile_cols))
    cb_b = jnp.asarray(_alibi_key_lanes(np_slopes_b, tile_cols))
    ind_ab = jnp.asarray(_alibi_query_lanes(tb), BF16)
    tq_c = min(TQ_CMP, s)
    cmp_end = np.arange(n_cmp) * CMP_STRIDE + (CMP_LEN - 1)
    cb_c = _alibi_key_lanes(np_slopes_c, cmp_end)[:, :, :6]
    cb_c = jnp.asarray(np.pad(cb_c.transpose(1, 0, 2).reshape(n_cmp, 6 * C_PAIRS),
                              ((0, 0), (0, LANE - 6 * C_PAIRS))))
    ind_c = np.zeros((C_PAIRS, 2 * tq_c, LANE), np.float32)
    for b in range(C_PAIRS):
        ind_c[b, :tq_c, 6 * b:6 * b + 3] = 1.0
        ind_c[b, tq_c:, 6 * b + 3:6 * b + 6] = 1.0
    ind_c = jnp.asarray(ind_c, BF16)
    ones_c = jnp.ones((n_cmp, LANE), BF16)

    def cg_spec(t):
        return pl.BlockSpec((None, t, LANE), lambda p, i, fl, sl: (BLK_CG, i, 0))

    eg_spec = pl.BlockSpec((C_PAIRS, LANE, LANE), lambda p, i, fl, sl: (0, 0, 0))
    cb_spec = pl.BlockSpec((1, tb, LANE), lambda p, i, fl, sl: (p, 0, 0))
    ind_spec = pl.BlockSpec((2 * tb, LANE), lambda p, i, fl, sl: (0, 0))

    dense_w = tuple(w.astype(BF16) for w in (ffn_w_gate, ffn_w_up, ffn_w_down))
    moe_w = tuple(w.astype(BF16).reshape((-1,) + w.shape[2:]) for w in (exp_w_gate, exp_w_up, exp_w_down))

    xs = x[0]
    for l in range(depth):
        w = (jnp.take(w_in[l], in_cols, axis=1) * col_scale[None, :]).astype(BF16)
        proj = _inproj(xs, ln_attn[l][None, :], w)

        lam_init = 0.8 - 0.6 * math.exp(-0.3 * l)
        lam_tab = jnp.zeros((8, LANE), F32).at[:4, :HEAD_DIM].set(
            jnp.stack([lam_q1[l], lam_k1[l], lam_q2[l], lam_k2[l]])).at[4, :].set(lam_init)
        oa = _flash(proj, slopes_a, tab_causal, mode="diff", qblk=BLK_AQ, kblk=BLK_AK, vblk=BLK_AV,
                    npi=1, n_outer=A_HEADS, kv_per_outer=True,
                    extra=(lam_tab, subln[l][None, :], cb_a, ind_ab),
                    extra_specs=(pl.BlockSpec((8, LANE), lambda p, i, fl, sl: (0, 0)),
                                 pl.BlockSpec((1, LANE), lambda p, i, fl, sl: (0, 0)),
                                 cb_spec, ind_spec))

        ob = _flash(proj, slopes_b, tab_dil, mode="pair", qblk=BLK_BQ, kblk=BLK_BK, vblk=BLK_BV,
                    npi=1, n_outer=B_HEADS // 2, kv_per_outer=True, max_ob=tab_dil.shape[0] - 1,
                    extra=(cb_b, ind_ab), extra_specs=(cb_spec, ind_spec))

        pc = proj[BLK_CKC:BLK_CVC + 1].reshape(2, n_cmp, CMP_STRIDE, C_GROUPS, HEAD_DIM)
        cin = pc.transpose(0, 3, 1, 2, 4).reshape(2 * C_GROUPS, n_cmp, CMP_STRIDE * HEAD_DIM)
        pos = jnp.stack([cmp_pos_k[l], cmp_pos_v[l]]).reshape(2, 1, CMP_LEN * HEAD_DIM)
        w1 = jnp.stack([cmp_w1_k[l], cmp_w1_v[l]]).astype(BF16)
        w2 = jnp.zeros((2, LANE, LANE), F32).at[:, :, :HEAD_DIM].set(
            jnp.stack([cmp_w2_k[l], cmp_w2_v[l]])).astype(BF16)
        cmp = _compress(cin, pos, w1, w2)
        kc = jnp.concatenate([cmp[0, :, :HEAD_DIM].astype(BF16), cmp[1, :, :HEAD_DIM].astype(BF16),
                              cb_c], axis=1)
        vc = jnp.concatenate([cmp[2, :, :HEAD_DIM].astype(BF16), cmp[3, :, :HEAD_DIM].astype(BF16),
                              ones_c], axis=1)
        oc, selm, anyblk = _cmp_select(proj, kc, vc, ind_c, eg[0], mtab)
        flags = anyblk[:, 0, :s // SLC_LEN].reshape(s // ts, -1, s // ts, ts // SLC_LEN)
        flags = flags.max(axis=(1, 3)).reshape(-1)
        os_ = _flash(proj, slopes_c, tab_causal_sel, mode="select", qblk=BLK_CQ, kblk=BLK_CKS, vblk=BLK_CVS,
                     npi=C_PAIRS, n_outer=1, kv_per_outer=False, flags=flags,
                     extra=(proj, eg[1], selm, ex_sel),
                     extra_specs=(cg_spec(ts), eg_spec,
                                  pl.BlockSpec((C_GROUPS, selm.shape[1], ts, LANE),
                                               lambda p, i, fl, sl: (0, 0, i, 0)),
                                  pl.BlockSpec(ex_sel.shape, lambda p, i, fl, sl: (0, 0, 0))))
        ow = _flash(proj, slopes_c, tab_win, mode="window", qblk=BLK_CQ, kblk=BLK_CKW, vblk=BLK_CVW,
                    npi=C_PAIRS, n_outer=1, kv_per_outer=False, max_ob=tab_win.shape[0] - 1,
                    extra=(proj, eg[2]), extra_specs=(cg_spec(tw), eg_spec))

        xs = _outproj(xs, oa, ob, oc, os_, ow, jnp.take(w_out[l], out_rows, axis=0).astype(BF16))

        j = l // 2
        g_ffn = ln_ffn[l][None, :]
        if l % 2 == 0:
            n_t = jnp.full((1,), one_tile.shape[0], jnp.int32)
            xs = _ffn(xs, g_ffn, *dense_w, one_tile + j, n_t, residual=True, tm=min(512, s))
        else:
            xs = _moe(xs, g_ffn, router_w[j], router_b[j], *moe_w, first_expert=j * N_EXPERTS)
    return _final_norm(xs, ln_final[None, :])[None]
```

```python
import functools
import math

import numpy as np
import jax
import jax.numpy as jnp
from jax import lax
from jax.experimental import pallas as pl
from jax.experimental.pallas import tpu as pltpu

F32 = jnp.float32
BF16 = jnp.bfloat16

HEAD_DIM = 64
LANE = 128
A_HEADS = 4
B_HEADS = 8
C_HEADS = 16
C_GROUPS = 2
C_PAIRS = C_HEADS // C_GROUPS
N_BRANCH = 3
CMP_LEN = 32
CMP_STRIDE = 16
SLC_LEN = 64
N_SELECT = 16
WIN = 512
DILATED_PATTERNS = ((128, 1), (512, 4), (2048, 16))
N_EXPERTS = 8
EPS = 1e-6
NEG = -1e30
FORCE_SCORE = 1e6
QK_SCALE = HEAD_DIM ** -0.5
LOG2E = math.log2(math.e)

BLK_AQ, BLK_AK, BLK_AV = 0, 4, 8
BLK_BQ, BLK_BK, BLK_BV = 12, 16, 20
BLK_CQ = 24
BLK_CKC, BLK_CVC, BLK_CKS, BLK_CVS, BLK_CKW, BLK_CVW, BLK_CG = 32, 33, 34, 35, 36, 37, 38
N_BLK = 40

T_BIG = 512
T_WIN = 256
T_SEL = 256
TQ_CMP = 128
TILE_UNROLL = 8
VMEM_LIMIT = 56 * 1024 * 1024


def _cparams(sem):
    return pltpu.CompilerParams(dimension_semantics=sem, vmem_limit_bytes=VMEM_LIMIT)


def _in_col_perm(d_mix):
    a_w = A_HEADS * HEAD_DIM
    sizes = [a_w, a_w, a_w, a_w, A_HEADS * 2 * HEAD_DIM,
             B_HEADS * HEAD_DIM, B_HEADS * HEAD_DIM, B_HEADS * HEAD_DIM,
             C_HEADS * HEAD_DIM] + [C_GROUPS * HEAD_DIM] * 6 + [C_HEADS * N_BRANCH]
    off = np.concatenate([[0], np.cumsum(sizes)])
    (aq1, aq2, ak1, ak2, av, bq, bk, bv, cq, ckc, cvc, cks, cvs, ckw, cvw, cg) = off[:-1]
    r = np.arange
    cols = []
    for h in range(A_HEADS):
        cols += [aq1 + h * 64 + r(64), aq2 + h * 64 + r(64)]
    for h in range(A_HEADS):
        cols += [ak1 + h * 64 + r(64), ak2 + h * 64 + r(64)]
    for h in range(A_HEADS):
        cols += [av + h * 128 + r(128)]
    for base in (bq, bk, bv):
        cols += [base + r(B_HEADS * HEAD_DIM)]
    for j in range(C_PAIRS):
        cols += [cq + j * 64 + r(64), cq + (C_PAIRS + j) * 64 + r(64)]
    for base in (ckc, cvc, cks, cvs, ckw, cvw):
        cols += [base + r(128)]
    cols += [cg + r(C_HEADS * N_BRANCH), -np.ones(LANE - C_HEADS * N_BRANCH, np.int64)]
    cols += [-np.ones(LANE, np.int64)]
    cols = np.concatenate(cols)
    assert cols.shape[0] == N_BLK * LANE and off[-1] == cols.max() + 1
    return cols


def _out_row_perm():
    rows = [np.arange(A_HEADS * 2 * HEAD_DIM + B_HEADS * HEAD_DIM)]
    c0 = rows[0].shape[0]
    for j in range(C_PAIRS):
        rows += [c0 + j * 64 + np.arange(64), c0 + (C_PAIRS + j) * 64 + np.arange(64)]
    return np.concatenate(rows)


def _alibi(n):
    return np.exp2(-8.0 * np.arange(1, n + 1, dtype=np.float64) / n).astype(np.float32)


def _causal_table(t):
    d = np.arange(t)[:, None] - np.arange(t)[None, :]
    return np.where(d >= 0, 0.0, NEG).astype(np.float32)[None]


def _dilated_table(t):
    n_ob = max(w for w, _ in DILATED_PATTERNS) // t + 1
    tabs = []
    for ob in range(n_ob):
        d = ob * t + np.arange(t)[:, None] - np.arange(t)[None, :]
        c = np.zeros(d.shape, np.float64)
        for w, dil in DILATED_PATTERNS:
            c += (d >= 0) & (d <= w) & (d % dil == 0)
        tabs.append(np.where(c > 0, np.log2(np.maximum(c, 1.0)), NEG))
    return np.stack(tabs).astype(np.float32)


def _window_table(t):
    n_ob = (WIN - 1 + t - 1) // t + 1
    tabs = []
    for ob in range(n_ob):
        d = ob * t + np.arange(t)[:, None] - np.arange(t)[None, :]
        tabs.append(np.where((d >= 0) & (d < WIN), 0.0, NEG))
    return np.stack(tabs).astype(np.float32)


def _select_expand_table(t):
    per = t // SLC_LEN
    n = LANE // per
    e = np.zeros((n, t, LANE), np.float32)
    for jj in range(n):
        e[jj, np.arange(t), jj * per + np.arange(t) // SLC_LEN] = 1e30
    return e


def _split3(v):
    v = np.asarray(v, np.float32)
    hi = v.astype(BF16)
    r = v - hi.astype(np.float32)
    mid = r.astype(BF16)
    lo = (r - mid.astype(np.float32)).astype(BF16)
    return hi, mid, lo


def _alibi_key_lanes(slopes2, cols):
    n_pairs = slopes2.shape[0] // 2
    v = (slopes2.astype(np.float32).reshape(n_pairs, 2, 1)
         * cols.astype(np.float32).reshape(1, 1, -1))
    parts = np.stack(_split3(v), axis=-1)
    lanes = parts.transpose(0, 2, 1, 3).reshape(n_pairs, -1, 6)
    return np.pad(lanes, ((0, 0), (0, 0), (0, LANE - 6)))


def _alibi_query_lanes(t):
    e = np.zeros((2 * t, LANE), np.float32)
    e[:t, 0:3] = 1.0
    e[t:, 3:6] = 1.0
    return e


def _gate_expand_table():
    e = np.zeros((N_BRANCH, C_PAIRS, LANE, LANE), np.float32)
    for br in range(N_BRANCH):
        for j in range(C_PAIRS):
            e[br, j, j * N_BRANCH + br, :64] = 1.0
            e[br, j, (C_PAIRS + j) * N_BRANCH + br, 64:] = 1.0
    return e


def _cmp_to_slc_table(n_cmp, n_slc):
    r = SLC_LEN // CMP_STRIDE
    m = np.zeros((n_cmp, n_slc), np.float32)
    for b in range(n_slc):
        for n in range(r * b - 1, r * b + r):
            if 0 <= n < n_cmp:
                m[n, b] = 1.0
    return m


def _rms(x, g):
    return x * lax.rsqrt(jnp.mean(x * x, axis=-1, keepdims=True) + EPS) * g


def _split_halves(q):
    lane = lax.broadcasted_iota(jnp.int32, q.shape, 1)
    zero = jnp.zeros_like(q)
    return jnp.concatenate([jnp.where(lane < HEAD_DIM, q, zero),
                            jnp.where(lane >= HEAD_DIM, q, zero)], axis=0)


def _merge_halves(o2, t):
    lane = lax.broadcasted_iota(jnp.int32, (t, LANE), 1)
    return jnp.where(lane < HEAD_DIM, o2[:t], o2[t:])


def _dot_nt(a, b):
    return lax.dot_general(a, b, (((1,), (1,)), ((), ())), preferred_element_type=F32)


def _dot_hilo(x, w):
    hi = x.astype(BF16)
    lo = (x - hi.astype(F32)).astype(BF16)
    return (jnp.dot(hi, w, preferred_element_type=F32)
            + jnp.dot(lo, w, preferred_element_type=F32))


def _gate_lanes(cg, eg):
    return _dot_hilo(jax.nn.sigmoid(cg.astype(F32)), eg)


def _inproj_kernel(x_ref, g_ref, w_ref, o_ref, h_ref, *, nb):
    @pl.when(pl.program_id(1) == 0)
    def _():
        h_ref[...] = _rms(x_ref[...], g_ref[...]).astype(BF16)

    h = h_ref[...]
    for c in range(nb // 2):
        r = jnp.dot(h, w_ref[:, c * 256:(c + 1) * 256], preferred_element_type=F32)
        o_ref[2 * c] = r[:, :LANE].astype(BF16)
        o_ref[2 * c + 1] = r[:, LANE:].astype(BF16)


def _inproj(x, g, w):
    s, d = x.shape
    tm, nb = 1024, 10
    tm = min(tm, s)
    return pl.pallas_call(
        functools.partial(_inproj_kernel, nb=nb),
        grid=(s // tm, N_BLK // nb),
        in_specs=[pl.BlockSpec((tm, d), lambda i, j: (i, 0)),
                  pl.BlockSpec((1, d), lambda i, j: (0, 0)),
                  pl.BlockSpec((d, nb * LANE), lambda i, j: (0, j))],
        out_specs=pl.BlockSpec((nb, tm, LANE), lambda i, j: (j, i, 0)),
        out_shape=jax.ShapeDtypeStruct((N_BLK, s, LANE), BF16),
        scratch_shapes=[pltpu.VMEM((tm, d), BF16)],
        compiler_params=_cparams(("parallel", "arbitrary")),
        name="inproj",
    )(x, g, w)


def _flash_kernel(*refs, mode, npi, t, nt, n_tab, max_ob):
    it = iter(refs)
    flag_ref, slope_ref = next(it), next(it)
    q_ref, k_ref, v_ref, tab_ref = next(it), next(it), next(it), next(it)
    if mode == "diff":
        lam_ref, gsub_ref = next(it), next(it)
    if mode in ("diff", "pair"):
        cb_ref, ind_ref = next(it), next(it)
    if mode in ("select", "window"):
        cg_ref, eg_ref = next(it), next(it)
    if mode == "select":
        sel_ref, ext_ref = next(it), next(it)
    o_ref = next(it)
    q2_ref, m_ref, acc_ref = next(it), next(it), next(it)

    alibi_in_matmul = mode in ("diff", "pair")
    p = pl.program_id(0)
    i = pl.program_id(1)
    rg = 2 * t
    reps = t // LANE
    for b in range(npi):
        q2 = _split_halves(q_ref[b])
        if alibi_in_matmul:
            q2 = jnp.concatenate([q2, ind_ref[...]], axis=1)
        q2_ref[b * rg:(b + 1) * rg] = q2
    m_ref[...] = jnp.full(m_ref.shape, NEG, F32)
    acc_ref[...] = jnp.zeros(acc_ref.shape, F32)
    ones = jnp.ones((t, LANE), BF16)

    def step(j, ob):
        ks = pl.multiple_of(j * t, t)
        k = k_ref[pl.ds(ks, t), :]
        vaug = jnp.concatenate([v_ref[pl.ds(ks, t), :], ones], axis=1)
        tile_off = ((j - i) * t).astype(F32)
        colv = lax.broadcasted_iota(jnp.int32, (1, t), 1).astype(F32) + tile_off
        if mode == "select":
            per = LANE * SLC_LEN // t
            k = jnp.concatenate([k, ext_ref[j % per]], axis=1)
            selcat = jnp.concatenate([sel_ref[0, j // per], sel_ref[1, j // per]], axis=0)
        for b in range(npi):
            qa = q2_ref[b * rg:(b + 1) * rg]
            if mode == "select":
                qa = jnp.concatenate([qa, selcat], axis=1)
            ka = jnp.concatenate([k, cb_ref[b]], axis=1) if alibi_in_matmul else k
            s = _dot_nt(qa, ka)
            ps, alphas = [], []
            for hf in range(2):
                g0 = b * rg + hf * t
                slope = slope_ref[(p * npi + b) * 2 + hf]
                sb = s[hf * t:(hf + 1) * t]
                if not alibi_in_matmul:
                    sb = sb + slope * colv
                if ob is not None:
                    sb = sb + tab_ref[ob]
                m_prev = m_ref[g0:g0 + t]
                row_max = jnp.max(sb, axis=1, keepdims=True)
                if alibi_in_matmul:
                    off = slope * tile_off
                    m_new = jnp.maximum(m_prev, row_max + off)
                    m_loc = m_new - off
                else:
                    m_new = jnp.maximum(m_prev, row_max)
                    m_loc = m_new
                alphas.append(jnp.exp2(m_prev - m_new))
                m_ref[g0:g0 + t] = m_new
                mb = jnp.concatenate([m_loc] * reps, axis=1)
                ps.append(jnp.exp2(sb - mb).astype(BF16))
            al = jnp.concatenate(alphas, axis=0)
            pv = jnp.dot(jnp.concatenate(ps, axis=0), vaug, preferred_element_type=F32)
            acc_ref[b * rg:(b + 1) * rg] = (jnp.concatenate([al, al], axis=1)
                                            * acc_ref[b * rg:(b + 1) * rg] + pv)

    j_lo = 0 if max_ob is None else jnp.maximum(i - max_ob, 0)
    j_mid = jnp.maximum(i - n_tab + 1, j_lo)
    if mode == "select":
        def body(j, c):
            @pl.when(flag_ref[i * nt + j] > 0)
            def _():
                step(j, None)
            return c

        lax.fori_loop(j_lo, j_mid, body, 0)
    elif max_ob is None or max_ob >= n_tab:
        n_full = jnp.right_shift(j_mid - j_lo, TILE_UNROLL.bit_length() - 1)

        def body_many(c, carry):
            for u in range(TILE_UNROLL):
                step(j_lo + TILE_UNROLL * c + u, None)
            return carry

        lax.fori_loop(0, n_full, body_many, 0)
        done = j_lo + TILE_UNROLL * n_full
        width = TILE_UNROLL // 2
        while width >= 1:
            @pl.when(jnp.bitwise_and(j_mid - j_lo, width) != 0)
            def _(done=done, width=width):
                for u in range(width):
                    step(done + u, None)

            done = done + jnp.bitwise_and(j_mid - j_lo, width)
            width //= 2
    if npi == 1 and n_tab > 1:
        @pl.when(i >= n_tab - 1)
        def _():
            for ob in reversed(range(n_tab)):
                step(i - ob, ob)
        early = n_tab - 1
    else:
        early = n_tab
    for ob in reversed(range(early)):
        @pl.when((i - ob >= 0) if early == n_tab else ((i - ob >= 0) & (i < n_tab - 1)))
        def _(ob=ob):
            step(i - ob, ob)

    for b in range(npi):
        a = acc_ref[b * rg:(b + 1) * rg]
        o2 = a[:, :LANE] / a[:, LANE:]
        if mode == "diff":
            lam_init = lam_ref[4:5, 0:1]
            lam = (jnp.exp(jnp.sum(lam_ref[0:1] * lam_ref[1:2], keepdims=True))
                   - jnp.exp(jnp.sum(lam_ref[2:3] * lam_ref[3:4], keepdims=True)) + lam_init)
            o = o2[:t] - lam * o2[t:]
            o_ref[b] = (_rms(o, gsub_ref[...]) * (1.0 - lam_init)).astype(BF16)
        else:
            o = _merge_halves(o2, t)
            if mode in ("select", "window"):
                o = o * _gate_lanes(cg_ref[...], eg_ref[b])
            o_ref[b] = o.astype(BF16)


def _flash(proj, slopes, tab, *, mode, qblk, kblk, vblk, npi, n_outer, kv_per_outer,
           max_ob=None, flags=None, extra=(), extra_specs=()):
    _, s, _ = proj.shape
    n_tab, t, _ = tab.shape
    rows = 2 * npi * t
    q_lanes = 2 * LANE if mode in ("diff", "pair") else LANE
    if flags is None:
        flags = jnp.zeros((1,), jnp.int32)
    kvmap = ((lambda b: (lambda p, i, fl, sl: (b + p, 0, 0))) if kv_per_outer
             else (lambda b: (lambda p, i, fl, sl: (b, 0, 0))))
    in_specs = [pl.BlockSpec((npi, t, LANE), lambda p, i, fl, sl: (qblk // npi + p, i, 0)),
                pl.BlockSpec((None, s, LANE), kvmap(kblk)),
                pl.BlockSpec((None, s, LANE), kvmap(vblk)),
                pl.BlockSpec(tab.shape, lambda p, i, fl, sl: (0, 0, 0))] + list(extra_specs)
    grid_spec = pltpu.PrefetchScalarGridSpec(
        num_scalar_prefetch=2,
        grid=(n_outer, s // t),
        in_specs=in_specs,
        out_specs=pl.BlockSpec((npi, t, LANE), lambda p, i, fl, sl: (p, i, 0)),
        scratch_shapes=[pltpu.VMEM((rows, q_lanes), BF16), pltpu.VMEM((rows, LANE), F32),
                        pltpu.VMEM((rows, 2 * LANE), F32)])
    return pl.pallas_call(
        functools.partial(_flash_kernel, mode=mode, npi=npi, t=t, nt=s // t, n_tab=n_tab,
                          max_ob=max_ob),
        grid_spec=grid_spec,
        out_shape=jax.ShapeDtypeStruct((n_outer * npi, s, LANE), BF16),
        compiler_params=_cparams(("parallel", "arbitrary")),
        name="flash_" + mode,
    )(flags, slopes, proj, proj, proj, tab, *extra)


def _compress_kernel(c_ref, pos_ref, w1_ref, w2_ref, o_ref):
    c = c_ref[0].astype(F32)
    half = c.shape[1]
    pos = pos_ref[0]
    xa = (c + pos[:, :half]).astype(BF16)
    xb = (c + pos[:, half:]).astype(BF16)
    u = jnp.dot(xa, w1_ref[0, :half], preferred_element_type=F32)
    v = jnp.dot(xb, w1_ref[0, half:], preferred_element_type=F32)
    hid = u + pltpu.roll(v, shift=v.shape[0] - 1, axis=0)
    act = hid * jax.nn.sigmoid(hid)
    o_ref[0] = jnp.dot(act.astype(BF16), w2_ref[0], preferred_element_type=F32)


def _compress(cin, pos, w1, w2):
    n, nc, width = cin.shape
    return pl.pallas_call(
        _compress_kernel,
        grid=(n,),
        in_specs=[pl.BlockSpec((1, nc, width), lambda a: (a, 0, 0)),
                  pl.BlockSpec((1, 1, 2 * width), lambda a: (a // 2, 0, 0)),
                  pl.BlockSpec((1, 2 * width, LANE), lambda a: (a // 2, 0, 0)),
                  pl.BlockSpec((1, LANE, LANE), lambda a: (a // 2, 0, 0))],
        out_specs=pl.BlockSpec((1, nc, LANE), lambda a: (a, 0, 0)),
        out_shape=jax.ShapeDtypeStruct((n, nc, LANE), F32),
        compiler_params=_cparams(("parallel",)),
        name="nsa_compress",
    )(cin, pos, w1, w2)


def _cmp_select_kernel(q_ref, kc_ref, vc_ref, ind_ref, cg_ref, eg_ref, m_ref,
                       o_ref, sel_ref, any_ref, slc_ref, *, tq, n_cmp, n_slc, n_sel, n_var):
    i = pl.program_id(0)
    t = i * tq + lax.broadcasted_iota(jnp.int32, (tq, 1), 0)
    row_ok = jnp.where(t >= CMP_LEN - 1, 1.0, 0.0)

    def attend(nc):
        cmp_end = lax.broadcasted_iota(jnp.int32, (1, nc), 1) * CMP_STRIDE + (CMP_LEN - 1)
        maskb = jnp.where(t >= cmp_end, 0.0, NEG)
        kc = kc_ref[:nc]
        vc = vc_ref[:nc]
        reps = nc // LANE
        imp = [jnp.zeros((tq, nc), F32), jnp.zeros((tq, nc), F32)]
        for b in range(C_PAIRS):
            qa = jnp.concatenate([_split_halves(q_ref[b]), ind_ref[b]], axis=1)
            s2 = _dot_nt(qa, kc)
            es = []
            for hf in range(2):
                s = s2[hf * tq:(hf + 1) * tq] + maskb
                es.append(jnp.exp2(s - jnp.max(s, axis=1, keepdims=True)))
            o2 = jnp.dot(jnp.concatenate(es, axis=0).astype(BF16), vc, preferred_element_type=F32)
            ok2 = jnp.concatenate([row_ok, row_ok], axis=0)
            rden = ok2 / o2[:, LANE:]
            for hf in range(2):
                r = rden[hf * tq:(hf + 1) * tq]
                imp[hf] = imp[hf] + es[hf] * jnp.concatenate([r] * reps, axis=1)
            o = _merge_halves(o2[:, :LANE] * rden, tq) * _gate_lanes(cg_ref[...], eg_ref[b])
            o_ref[b] = o.astype(BF16)
        imp2 = jnp.concatenate(imp, axis=0)
        imp_hi = imp2.astype(BF16)
        imp_lo = (imp2 - imp_hi.astype(F32)).astype(BF16)
        slc_ref[...] = _dot_nt(m_ref[:, :nc], imp_hi) + _dot_nt(m_ref[:, :nc], imp_lo)

    n_q = pl.num_programs(0)
    for v in range(1, n_var + 1):
        @pl.when((i * n_var >= (v - 1) * n_q) & (i * n_var < v * n_q))
        def _(v=v):
            attend(n_cmp * v // n_var)

    slc = slc_ref[...]
    t_row = i * tq + lax.broadcasted_iota(jnp.int32, (1, tq), 1)
    t2 = jnp.concatenate([t_row, t_row], axis=1)
    blk = lax.broadcasted_iota(jnp.int32, (n_slc, 2 * tq), 0)
    tb = jnp.right_shift(t2, int(math.log2(SLC_LEN)))
    forced = (blk == 0) | (blk == tb) | (blk == tb - 1)
    allowed = blk * SLC_LEN <= t2
    score = jnp.where(forced, -2.0, jnp.where(allowed, slc, -1.0))
    picked = jnp.where(forced, 1, 0)
    for _ in range(max(n_sel - 3, 0)):
        mx = jnp.max(score, axis=0, keepdims=True)
        first = jnp.min(jnp.where(score == mx, blk, n_slc), axis=0, keepdims=True)
        hit = blk == first
        picked = jnp.where(hit, 1, picked)
        score = jnp.where(hit, -2.0, score)
    sel_t = jnp.where(picked > 0, 0.0, -1.0)
    any_t = jnp.where(allowed & (picked > 0), 1.0, 0.0)
    both_t = jnp.concatenate([sel_t, any_t], axis=0).astype(BF16)
    rr = lax.broadcasted_iota(jnp.int32, (2 * tq, 2 * tq), 0)
    cc = lax.broadcasted_iota(jnp.int32, (2 * tq, 2 * tq), 1)
    eye = jnp.where(rr == cc, 1.0, 0.0).astype(BF16)
    both = _dot_nt(eye, both_t)
    selm = both[:, :n_slc].astype(BF16)
    for g in range(C_GROUPS):
        for hf in range(n_slc // LANE):
            sel_ref[g, hf] = selm[g * tq:(g + 1) * tq, hf * LANE:(hf + 1) * LANE]
    any_ref[0] = jnp.max(both[:, n_slc:], axis=0, keepdims=True).astype(jnp.int32)


def _cmp_select(proj, kc, vc, ind, eg, mtab):
    _, s, _ = proj.shape
    tq = ind.shape[1] // 2
    n_cmp = kc.shape[0]
    n_slc = mtab.shape[0]
    n_var = 4 if (n_cmp % (4 * LANE) == 0 and (s // tq) % 4 == 0) else 1
    return pl.pallas_call(
        functools.partial(_cmp_select_kernel, tq=tq, n_cmp=n_cmp, n_slc=n_slc,
                          n_sel=min(N_SELECT, s // SLC_LEN), n_var=n_var),
        grid=(s // tq,),
        in_specs=[pl.BlockSpec((C_PAIRS, tq, LANE), lambda i: (BLK_CQ // C_PAIRS, i, 0)),
                  pl.BlockSpec((n_cmp, 2 * LANE), lambda i: (0, 0)),
                  pl.BlockSpec((n_cmp, 2 * LANE), lambda i: (0, 0)),
                  pl.BlockSpec(ind.shape, lambda i: (0, 0, 0)),
                  pl.BlockSpec((None, tq, LANE), lambda i: (BLK_CG, i, 0)),
                  pl.BlockSpec((C_PAIRS, LANE, LANE), lambda i: (0, 0, 0)),
                  pl.BlockSpec((n_slc, n_cmp), lambda i: (0, 0))],
        out_specs=[pl.BlockSpec((C_PAIRS, tq, LANE), lambda i: (0, i, 0)),
                   pl.BlockSpec((C_GROUPS, n_slc // LANE, tq, LANE), lambda i: (0, 0, i, 0)),
                   pl.BlockSpec((1, 1, n_slc), lambda i: (i, 0, 0))],
        out_shape=[jax.ShapeDtypeStruct((C_PAIRS, s, LANE), BF16),
                   jax.ShapeDtypeStruct((C_GROUPS, n_slc // LANE, s, LANE), BF16),
                   jax.ShapeDtypeStruct((s // tq, 1, n_slc), jnp.int32)],
        scratch_shapes=[pltpu.VMEM((n_slc, 2 * tq), F32)],
        compiler_params=_cparams(("parallel",)),
        name="nsa_cmp_select",
    )(proj, kc, vc, ind, proj, eg, mtab)


def _outproj_kernel(x_ref, oa_ref, ob_ref, oc_ref, os_ref, ow_ref, w_ref, o_ref):
    parts = [oa_ref[b] for b in range(oa_ref.shape[0])]
    parts += [ob_ref[b] for b in range(ob_ref.shape[0])]
    for b in range(oc_ref.shape[0]):
        parts.append((oc_ref[b].astype(F32) + os_ref[b].astype(F32)
                      + ow_ref[b].astype(F32)).astype(BF16))
    mix = jnp.concatenate(parts, axis=1)
    n = o_ref.shape[1]
    for c in range(n // 512):
        sl = slice(c * 512, (c + 1) * 512)
        o_ref[:, sl] = x_ref[:, sl] + jnp.dot(mix, w_ref[:, sl], preferred_element_type=F32)


def _outproj(x, oa, ob, oc, os_, ow, w):
    s, d = x.shape
    tm = min(512, s)
    blk = lambda a: pl.BlockSpec((a.shape[0], tm, LANE), lambda i: (0, i, 0))
    return pl.pallas_call(
        _outproj_kernel,
        grid=(s // tm,),
        in_specs=[pl.BlockSpec((tm, d), lambda i: (i, 0)), blk(oa), blk(ob), blk(oc), blk(os_), blk(ow),
                  pl.BlockSpec(w.shape, lambda i: (0, 0))],
        out_specs=pl.BlockSpec((tm, d), lambda i: (i, 0)),
        out_shape=jax.ShapeDtypeStruct((s, d), F32),
        compiler_params=_cparams(("parallel",)),
        name="outproj",
    )(x, oa, ob, oc, os_, ow, w)


def _ffn_kernel(te_ref, nt_ref, x_ref, g_ref, wg_ref, wu_ref, wd_ref, o_ref, h_ref, z_ref, acc_ref,
                *, residual):
    t = pl.program_id(0)
    j = pl.program_id(1)
    n_f = pl.num_programs(1) - 1
    live = t < nt_ref[0]

    def gate_up():
        h = h_ref[...]
        a = jnp.dot(h, wg_ref[0], preferred_element_type=F32)
        u = jnp.dot(h, wu_ref[0], preferred_element_type=F32)
        return (a * jax.nn.sigmoid(a) * u).astype(BF16)

    @pl.when(live & (j == 0))
    def _():
        h_ref[...] = _rms(x_ref[...], g_ref[...]).astype(BF16)
        acc_ref[...] = jnp.zeros(acc_ref.shape, F32)
        z_ref[...] = gate_up()

    @pl.when(live & (j > 0) & (j < n_f))
    def _():
        acc_ref[...] += jnp.dot(z_ref[...], wd_ref[0], preferred_element_type=F32)
        z_ref[...] = gate_up()

    @pl.when(live & (j == n_f))
    def _():
        out = acc_ref[...] + jnp.dot(z_ref[...], wd_ref[0], preferred_element_type=F32)
        o_ref[...] = x_ref[...] + out if residual else out

    @pl.when(jnp.logical_not(live))
    def _():
        o_ref[...] = jnp.zeros(o_ref.shape, F32)


def _ffn(x, g, wg, wu, wd, tile_expert, n_tiles, *, residual, tm):
    r, d = x.shape
    f = wg.shape[2]
    tf = 512 if f % 512 == 0 else f
    n_f = f // tf

    def gu_map(t, j, te, nt):
        return te[t], 0, jnp.where(t < nt[0], jnp.minimum(j, n_f - 1), n_f - 1)

    def down_map(t, j, te, nt):
        return te[t], jnp.where(t < nt[0], jnp.maximum(j - 1, 0), n_f - 1), 0

    grid_spec = pltpu.PrefetchScalarGridSpec(
        num_scalar_prefetch=2,
        grid=(r // tm, n_f + 1),
        in_specs=[pl.BlockSpec((tm, d), lambda t, j, te, nt: (t, 0)),
                  pl.BlockSpec((1, d), lambda t, j, te, nt: (0, 0)),
                  pl.BlockSpec((1, d, tf), gu_map),
                  pl.BlockSpec((1, d, tf), gu_map),
                  pl.BlockSpec((1, tf, d), down_map)],
        out_specs=pl.BlockSpec((tm, d), lambda t, j, te, nt: (t, 0)),
        scratch_shapes=[pltpu.VMEM((tm, d), BF16), pltpu.VMEM((tm, tf), BF16),
                        pltpu.VMEM((tm, d), F32)])
    return pl.pallas_call(
        functools.partial(_ffn_kernel, residual=residual),
        grid_spec=grid_spec,
        out_shape=jax.ShapeDtypeStruct((r, d), F32),
        compiler_params=_cparams(("parallel", "arbitrary")),
        name="ffn",
    )(tile_expert, n_tiles, x, g, wg, wu, wd)


def _router_kernel(x_ref, g_ref, w_ref, b_ref, o_ref):
    h = _rms(x_ref[...], g_ref[...])
    w = w_ref[...]
    w_hi = w.astype(BF16)
    w_lo = (w - w_hi.astype(F32)).astype(BF16)
    h_hi = h.astype(BF16)
    h_lo = (h - h_hi.astype(F32)).astype(BF16)
    logits = (jnp.dot(h_hi, w_hi, preferred_element_type=F32)
              + jnp.dot(h_lo, w_hi, preferred_element_type=F32)
              + jnp.dot(h_hi, w_lo, preferred_element_type=F32)) + b_ref[...]
    lane = lax.broadcasted_iota(jnp.int32, logits.shape, 1)
    logits = jnp.where(lane < N_EXPERTS, logits, NEG)
    l1 = jnp.max(logits, axis=1, keepdims=True)
    i1 = jnp.min(jnp.where(logits == l1, lane, LANE), axis=1, keepdims=True)
    rest = jnp.where(lane == i1, NEG, logits)
    l2 = jnp.max(rest, axis=1, keepdims=True)
    i2 = jnp.min(jnp.where(rest == l2, lane, LANE), axis=1, keepdims=True)
    e2 = jnp.exp(l2 - l1)
    w1 = 1.0 / (1.0 + e2)
    w2 = e2 / (1.0 + e2)
    out = jnp.where(lane == 0, i1.astype(F32),
                    jnp.where(lane == 1, i2.astype(F32),
                              jnp.where(lane == 2, w1, jnp.where(lane == 3, w2, 0.0))))
    o_ref[...] = out


def _router(x, g, w, b):
    s, d = x.shape
    tm = min(512, s)
    return pl.pallas_call(
        _router_kernel,
        grid=(s // tm,),
        in_specs=[pl.BlockSpec((tm, d), lambda i: (i, 0)),
                  pl.BlockSpec((1, d), lambda i: (0, 0)),
                  pl.BlockSpec((d, LANE), lambda i: (0, 0)),
                  pl.BlockSpec((1, LANE), lambda i: (0, 0))],
        out_specs=pl.BlockSpec((tm, LANE), lambda i: (i, 0)),
        out_shape=jax.ShapeDtypeStruct((s, LANE), F32),
        compiler_params=_cparams(("parallel",)),
        name="router",
    )(x, g, w, b)


def _row_copy(src_hbm, row, dst_ref, r, sem):
    return pltpu.make_async_copy(src_hbm.at[pl.ds(row, 1), :], dst_ref.at[pl.ds(r, 1), :], sem)


ISSUE_UNROLL = 8


def _gather_kernel(idx_ref, x_hbm, o_ref, sem, *, tm):
    base = pl.program_id(0) * tm

    def issue(c, carry):
        for u in range(ISSUE_UNROLL):
            r = c * ISSUE_UNROLL + u
            _row_copy(x_hbm, idx_ref[base + r], o_ref, r, sem).start()
        return carry

    lax.fori_loop(0, tm // ISSUE_UNROLL, issue, 0)

    def wait(r, c):
        _row_copy(x_hbm, 0, o_ref, r, sem).wait()
        return c

    lax.fori_loop(0, tm, wait, 0)


def _gather_rows(x, idx, tm):
    d = x.shape[1]
    n = idx.shape[0]
    grid_spec = pltpu.PrefetchScalarGridSpec(
        num_scalar_prefetch=1,
        grid=(n // tm,),
        in_specs=[pl.BlockSpec(memory_space=pl.ANY)],
        out_specs=pl.BlockSpec((tm, d), lambda i, idx: (i, 0)),
        scratch_shapes=[pltpu.SemaphoreType.DMA(())])
    return pl.pallas_call(
        functools.partial(_gather_kernel, tm=tm),
        grid_spec=grid_spec,
        out_shape=jax.ShapeDtypeStruct((n, d), x.dtype),
        compiler_params=_cparams(("arbitrary",)),
        name="moe_gather",
    )(idx, x)


def _combine_kernel(d0_ref, d1_ref, x_ref, r_ref, y_hbm, o_ref, b0_ref, b1_ref, sem, *, tm):
    base = pl.program_id(0) * tm

    def issue(c, carry):
        for u in range(ISSUE_UNROLL):
            r = c * ISSUE_UNROLL + u
            _row_copy(y_hbm, d0_ref[base + r], b0_ref, r, sem).start()
            _row_copy(y_hbm, d1_ref[base + r], b1_ref, r, sem).start()
        return carry

    lax.fori_loop(0, tm // ISSUE_UNROLL, issue, 0)

    def wait(r, c):
        _row_copy(y_hbm, 0, b0_ref, r, sem).wait()
        _row_copy(y_hbm, 0, b1_ref, r, sem).wait()
        return c

    lax.fori_loop(0, tm, wait, 0)
    rt = r_ref[...]
    o_ref[...] = x_ref[...] + rt[:, 2:3] * b0_ref[...] + rt[:, 3:4] * b1_ref[...]


def _combine(x, route, y, d0, d1):
    s, d = x.shape
    tm = min(256, s)
    grid_spec = pltpu.PrefetchScalarGridSpec(
        num_scalar_prefetch=2,
        grid=(s // tm,),
        in_specs=[pl.BlockSpec((tm, d), lambda i, a, b: (i, 0)),
                  pl.BlockSpec((tm, LANE), lambda i, a, b: (i, 0)),
                  pl.BlockSpec(memory_space=pl.ANY)],
        out_specs=pl.BlockSpec((tm, d), lambda i, a, b: (i, 0)),
        scratch_shapes=[pltpu.VMEM((tm, d), F32), pltpu.VMEM((tm, d), F32),
                        pltpu.SemaphoreType.DMA(())])
    return pl.pallas_call(
        functools.partial(_combine_kernel, tm=tm),
        grid_spec=grid_spec,
        out_shape=jax.ShapeDtypeStruct((s, d), F32),
        compiler_params=_cparams(("arbitrary",)),
        name="moe_combine",
    )(d0, d1, x, route, y)


def _moe(x, g, w_router, b_router, wg, wu, wd, *, first_expert):
    s, d = x.shape
    tm = min(512, s)
    wr = jnp.zeros((d, LANE), F32).at[:, :N_EXPERTS].set(w_router)
    br = jnp.zeros((1, LANE), F32).at[0, :N_EXPERTS].set(b_router)
    route = _router(x, g, wr, br)
    e = route[:, :2].astype(jnp.int32).reshape(-1)
    onehot = (e[:, None] == jnp.arange(N_EXPERTS)[None, :]).astype(jnp.int32)
    rank = jnp.sum((jnp.cumsum(onehot, axis=0) - onehot) * onehot, axis=1)
    counts = jnp.sum(onehot, axis=0)
    padded = ((counts + tm - 1) // tm) * tm
    ends = jnp.cumsum(padded)
    dest = (ends - padded)[e] + rank
    n_rows = 2 * s + N_EXPERTS * tm
    row_token = jnp.zeros((n_rows,), jnp.int32).at[dest].set(jnp.arange(2 * s, dtype=jnp.int32) // 2)
    tile_start = jnp.arange(n_rows // tm, dtype=jnp.int32) * tm
    tile_expert = jnp.minimum(jnp.sum(tile_start[:, None] >= ends[None, :], axis=1), N_EXPERTS - 1)
    n_tiles = (ends[-1] // tm).astype(jnp.int32).reshape(1)
    xs = _gather_rows(x, row_token, tm)
    ys = _ffn(xs, g, wg, wu, wd, tile_expert.astype(jnp.int32) + first_expert, n_tiles,
              residual=False, tm=tm)
    dest2 = dest.reshape(s, 2).astype(jnp.int32)
    return _combine(x, route, ys, dest2[:, 0], dest2[:, 1])


def _norm_kernel(x_ref, g_ref, o_ref):
    o_ref[...] = _rms(x_ref[...], g_ref[...])


def _final_norm(x, g):
    s, d = x.shape
    tm = min(512, s)
    return pl.pallas_call(
        _norm_kernel,
        grid=(s // tm,),
        in_specs=[pl.BlockSpec((tm, d), lambda i: (i, 0)), pl.BlockSpec((1, d), lambda i: (0, 0))],
        out_specs=pl.BlockSpec((tm, d), lambda i: (i, 0)),
        out_shape=jax.ShapeDtypeStruct((s, d), F32),
        compiler_params=_cparams(("parallel",)),
        name="final_norm",
    )(x, g)


def kernel(x, ln_attn, w_in, w_out, lam_q1, lam_k1, lam_q2, lam_k2, subln,
           cmp_pos_k, cmp_w1_k, cmp_w2_k, cmp_pos_v, cmp_w1_v, cmp_w2_v,
           ln_ffn, ffn_w_gate, ffn_w_up, ffn_w_down,
           router_w, router_b, exp_w_gate, exp_w_up, exp_w_down, ln_final):
    bsz, s, d = x.shape
    assert bsz == 1 and s % T_BIG == 0
    depth = w_in.shape[0]
    tb, tw = T_BIG, T_WIN
    n_cmp = s // CMP_STRIDE
    nt = s // tb

    in_cols = _in_col_perm(d)
    col_scale = (in_cols >= 0).astype(np.float32)
    for q0, nq in ((BLK_AQ, A_HEADS), (BLK_BQ, B_HEADS // 2), (BLK_CQ, C_PAIRS)):
        col_scale[q0 * LANE:(q0 + nq) * LANE] *= QK_SCALE * LOG2E
    col_scale = jnp.asarray(col_scale)
    in_cols = jnp.asarray(np.maximum(in_cols, 0))
    out_rows = jnp.asarray(_out_row_perm())

    np_slopes_a = (np.repeat(_alibi(A_HEADS), 2) * LOG2E).astype(np.float32)
    np_slopes_b = (_alibi(B_HEADS) * LOG2E).astype(np.float32)
    sl_c = _alibi(C_HEADS) * LOG2E
    np_slopes_c = np.stack([sl_c[:C_PAIRS], sl_c[C_PAIRS:]], axis=1).reshape(-1).astype(np.float32)
    slopes_a, slopes_b, slopes_c = (jnp.asarray(v) for v in (np_slopes_a, np_slopes_b, np_slopes_c))
    tab_causal = jnp.asarray(_causal_table(tb))
    tab_dil = jnp.asarray(_dilated_table(tb))
    tab_win = jnp.asarray(_window_table(tw))
    ts = T_SEL
    tab_causal_sel = jnp.asarray(_causal_table(ts))
    ex_sel = jnp.asarray(_select_expand_table(ts), BF16)
    eg = jnp.asarray(_gate_expand_table(), BF16)
    n_slc_pad = -(-(s // SLC_LEN) // LANE) * LANE
    mtab = jnp.asarray(_cmp_to_slc_table(n_cmp, n_slc_pad).T, BF16)
    one_tile = jnp.zeros((s // min(512, s),), jnp.int32)
    tile_cols = np.arange(tb)
    cb_a = jnp.asarray(_alibi_key_lanes(np_slopes_a, tile_cols))
    cb_b = jnp.asarray(_alibi_key_lanes(np_slopes_b, tile_cols))
    ind_ab = jnp.asarray(_alibi_query_lanes(tb), BF16)
    tq_c = min(TQ_CMP, s)
    cmp_end = np.arange(n_cmp) * CMP_STRIDE + (CMP_LEN - 1)
    cb_c = _alibi_key_lanes(np_slopes_c, cmp_end)[:, :, :6]
    cb_c = jnp.asarray(np.pad(cb_c.transpose(1, 0, 2).reshape(n_cmp, 6 * C_PAIRS),
                              ((0, 0), (0, LANE - 6 * C_PAIRS))))
    ind_c = np.zeros((C_PAIRS, 2 * tq_c, LANE), np.float32)
    for b in range(C_PAIRS):
        ind_c[b, :tq_c, 6 * b:6 * b + 3] = 1.0
        ind_c[b, tq_c:, 6 * b + 3:6 * b + 6] = 1.0
    ind_c = jnp.asarray(ind_c, BF16)
    ones_c = jnp.ones((n_cmp, LANE), BF16)

    def cg_spec(t):
        return pl.BlockSpec((None, t, LANE), lambda p, i, fl, sl: (BLK_CG, i, 0))

    eg_spec = pl.BlockSpec((C_PAIRS, LANE, LANE), lambda p, i, fl, sl: (0, 0, 0))
    cb_spec = pl.BlockSpec((1, tb, LANE), lambda p, i, fl, sl: (p, 0, 0))
    ind_spec = pl.BlockSpec((2 * tb, LANE), lambda p, i, fl, sl: (0, 0))

    dense_w = tuple(w.astype(BF16) for w in (ffn_w_gate, ffn_w_up, ffn_w_down))
    moe_w = tuple(w.astype(BF16).reshape((-1,) + w.shape[2:]) for w in (exp_w_gate, exp_w_up, exp_w_down))

    xs = x[0]
    for l in range(depth):
        w = (jnp.take(w_in[l], in_cols, axis=1) * col_scale[None, :]).astype(BF16)
        proj = _inproj(xs, ln_attn[l][None, :], w)

        lam_init = 0.8 - 0.6 * math.exp(-0.3 * l)
        lam_tab = jnp.zeros((8, LANE), F32).at[:4, :HEAD_DIM].set(
            jnp.stack([lam_q1[l], lam_k1[l], lam_q2[l], lam_k2[l]])).at[4, :].set(lam_init)
        oa = _flash(proj, slopes_a, tab_causal, mode="diff", qblk=BLK_AQ, kblk=BLK_AK, vblk=BLK_AV,
                    npi=1, n_outer=A_HEADS, kv_per_outer=True,
                    extra=(lam_tab, subln[l][None, :], cb_a, ind_ab),
                    extra_specs=(pl.BlockSpec((8, LANE), lambda p, i, fl, sl: (0, 0)),
                                 pl.BlockSpec((1, LANE), lambda p, i, fl, sl: (0, 0)),
                                 cb_spec, ind_spec))

        ob = _flash(proj, slopes_b, tab_dil, mode="pair", qblk=BLK_BQ, kblk=BLK_BK, vblk=BLK_BV,
                    npi=1, n_outer=B_HEADS // 2, kv_per_outer=True, max_ob=tab_dil.shape[0] - 1,
                    extra=(cb_b, ind_ab), extra_specs=(cb_spec, ind_spec))

        pc = proj[BLK_CKC:BLK_CVC + 1].reshape(2, n_cmp, CMP_STRIDE, C_GROUPS, HEAD_DIM)
        cin = pc.transpose(0, 3, 1, 2, 4).reshape(2 * C_GROUPS, n_cmp, CMP_STRIDE * HEAD_DIM)
        pos = jnp.stack([cmp_pos_k[l], cmp_pos_v[l]]).reshape(2, 1, CMP_LEN * HEAD_DIM)
        w1 = jnp.stack([cmp_w1_k[l], cmp_w1_v[l]]).astype(BF16)
        w2 = jnp.zeros((2, LANE, LANE), F32).at[:, :, :HEAD_DIM].set(
            jnp.stack([cmp_w2_k[l], cmp_w2_v[l]])).astype(BF16)
        cmp = _compress(cin, pos, w1, w2)
        kc = jnp.concatenate([cmp[0, :, :HEAD_DIM].astype(BF16), cmp[1, :, :HEAD_DIM].astype(BF16),
                              cb_c], axis=1)
        vc = jnp.concatenate([cmp[2, :, :HEAD_DIM].astype(BF16), cmp[3, :, :HEAD_DIM].astype(BF16),
                              ones_c], axis=1)
        oc, selm, anyblk = _cmp_select(proj, kc, vc, ind_c, eg[0], mtab)
        flags = anyblk[:, 0, :s // SLC_LEN].reshape(s // ts, -1, s // ts, ts // SLC_LEN)
        flags = flags.max(axis=(1, 3)).reshape(-1)
        os_ = _flash(proj, slopes_c, tab_causal_sel, mode="select", qblk=BLK_CQ, kblk=BLK_CKS, vblk=BLK_CVS,
                     npi=C_PAIRS, n_outer=1, kv_per_outer=False, flags=flags,
                     extra=(proj, eg[1], selm, ex_sel),
                     extra_specs=(cg_spec(ts), eg_spec,
                                  pl.BlockSpec((C_GROUPS, selm.shape[1], ts, LANE),
                                               lambda p, i, fl, sl: (0, 0, i, 0)),
                                  pl.BlockSpec(ex_sel.shape, lambda p, i, fl, sl: (0, 0, 0))))
        ow = _flash(proj, slopes_c, tab_win, mode="window", qblk=BLK_CQ, kblk=BLK_CKW, vblk=BLK_CVW,
                    npi=C_PAIRS, n_outer=1, kv_per_outer=False, max_ob=tab_win.shape[0] - 1,
                    extra=(proj, eg[2]), extra_specs=(cg_spec(tw), eg_spec))

        xs = _outproj(xs, oa, ob, oc, os_, ow, jnp.take(w_out[l], out_rows, axis=0).astype(BF16))

        j = l // 2
        g_ffn = ln_ffn[l][None, :]
        if l % 2 == 0:
            n_t = jnp.full((1,), one_tile.shape[0], jnp.int32)
            xs = _ffn(xs, g_ffn, *dense_w, one_tile + j, n_t, residual=True, tm=min(512, s))
        else:
            xs = _moe(xs, g_ffn, router_w[j], router_b[j], *moe_w, first_expert=j * N_EXPERTS)
    return _final_norm(xs, ln_final[None, :])[None]
```
